```python
import math
import jax, jax.numpy as jnp
from jax import lax
import numpy as np

D_MODEL = 1024
BATCH = 16
SEQ = 2048
DEPTH = 1

MEM_LEN = 256
EPS = 1e-6

SSD_HEADS = 16
SSD_HEAD_DIM = 64
SSD_INNER = SSD_HEADS * SSD_HEAD_DIM
SSD_GROUPS = 2
SSD_STATE = 128
SSD_CONV = 4
SSD_CHUNK = 128
SSD_CONV_DIM = SSD_INNER + 2 * SSD_GROUPS * SSD_STATE
DT_MIN = 0.001
DT_MAX = 0.1

GMLP_GROUPS = 8
GMLP_INNER = 1024
GMLP_GROUP_DIM = GMLP_INNER // GMLP_GROUPS
GMLP_CHUNK = 128

XA_HEADS = 4
XA_HEAD_DIM = 256
XA_INNER = XA_HEADS * XA_HEAD_DIM

N_BRANCH = 3
BRANCH_WIDTH = 1024

IN_DIM = SSD_INNER + SSD_CONV_DIM + SSD_HEADS + 2 * GMLP_INNER + XA_INNER + N_BRANCH * D_MODEL

PEER_HEADS = 8
PEER_N_KEYS = 128
PEER_N_EXPERTS = PEER_N_KEYS * PEER_N_KEYS
PEER_QUERY_DIM = 256
PEER_HALF = PEER_QUERY_DIM // 2
PEER_TOPK = 16
PEER_TOKEN_BLOCK = 128

kernel_name = 'hybrid_ssd_gmlp_memxattn_peer'


def rms_norm(x, g):
    xf = x.astype(jnp.float32)
    y = xf * lax.rsqrt(jnp.mean(xf * xf, axis=-1, keepdims=True) + EPS)
    return (y * g.astype(jnp.float32)).astype(x.dtype)


def causal_depthwise_conv(x, w, b):
    k = w.shape[0]
    y = lax.conv_general_dilated(
        x, w[:, None, :].astype(x.dtype), window_strides=(1,), padding=[(k - 1, 0)],
        dimension_numbers=('NWC', 'WIO', 'NWC'), feature_group_count=x.shape[-1])
    return y + b.astype(x.dtype)


def ssd_chunked(xh, dt, a, bm, cm):
    out_dtype = xh.dtype
    b, s = xh.shape[0], xh.shape[1]
    c, l = s // SSD_CHUNK, SSD_CHUNK
    g, r = SSD_GROUPS, SSD_HEADS // SSD_GROUPS
    xdt = (xh.astype(jnp.float32) * dt[..., None]).reshape(b, c, l, g, r, SSD_HEAD_DIM)
    adt = (dt * a).reshape(b, c, l, g, r).transpose(0, 3, 4, 1, 2)
    bc = bm.astype(jnp.float32).reshape(b, c, l, g, SSD_STATE)
    cc = cm.astype(jnp.float32).reshape(b, c, l, g, SSD_STATE)
    a_cs = jnp.cumsum(adt, axis=-1)
    causal = jnp.tril(jnp.ones((l, l), dtype=bool))
    seg = a_cs[..., :, None] - a_cs[..., None, :]
    lmat = jnp.exp(jnp.where(causal, seg, -jnp.inf))
    cb = jnp.einsum('bclgn,bcsgn->bcgls', cc, bc)
    y_diag = jnp.einsum('bcgls,bgrcls,bcsgrp->bclgrp', cb, lmat, xdt)
    decay = jnp.exp(a_cs[..., -1:] - a_cs)
    states = jnp.einsum('bclgn,bgrcl,bclgrp->cbgrpn', bc, decay, xdt)
    chunk_decay = jnp.exp(a_cs[..., -1]).transpose(3, 0, 1, 2)

    def step(carry, inp):
        st, dec = inp
        return carry * dec[..., None, None] + st, carry

    init = jnp.zeros(states.shape[1:], jnp.float32)
    _, prev = lax.scan(step, init, (states, chunk_decay))
    y_off = jnp.einsum('bclgn,cbgrpn,bgrcl->bclgrp', cc, prev, jnp.exp(a_cs))
    return (y_diag + y_off).reshape(b, s, SSD_HEADS, SSD_HEAD_DIM).astype(out_dtype)


def ssd_branch(xbc_raw, z, dt_raw, conv_w, conv_b, dt_bias, a_log, d_skip, ssd_norm):
    b, s = z.shape[0], z.shape[1]
    xbc = jax.nn.silu(causal_depthwise_conv(xbc_raw, conv_w, conv_b))
    xs, bm, cm = jnp.split(xbc, [SSD_INNER, SSD_INNER + SSD_GROUPS * SSD_STATE], axis=-1)
    xh = xs.reshape(b, s, SSD_HEADS, SSD_HEAD_DIM)
    dt = jax.nn.softplus(dt_raw.astype(jnp.float32) + dt_bias.astype(jnp.float32))
    a = -jnp.exp(a_log.astype(jnp.float32))
    y = ssd_chunked(xh, dt, a,
                    bm.reshape(b, s, SSD_GROUPS, SSD_STATE),
                    cm.reshape(b, s, SSD_GROUPS, SSD_STATE))
    y = (y + d_skip.astype(xh.dtype)[:, None] * xh).reshape(b, s, SSD_INNER)
    return rms_norm(y * jax.nn.silu(z), ssd_norm)


def gmlp_branch(u, v, gmlp_norm, w_spatial, b_spatial):
    b, s = u.shape[0], u.shape[1]
    u = jax.nn.gelu(u)
    v = rms_norm(jax.nn.gelu(v), gmlp_norm)
    vc = v.reshape(b, s // GMLP_CHUNK, GMLP_CHUNK, GMLP_GROUPS, GMLP_GROUP_DIM)
    w = w_spatial * jnp.tril(jnp.ones((GMLP_CHUNK, GMLP_CHUNK), w_spatial.dtype))
    mixed = jnp.einsum('gts,bcsgk->bctgk', w, vc) + b_spatial.T[:, :, None]
    return u * mixed.reshape(b, s, GMLP_INNER)


def memory_cross_attention(q, mem_n, w_mem_kv):
    b, s = q.shape[0], q.shape[1]
    m = mem_n.shape[1]
    k, v = jnp.split(mem_n @ w_mem_kv, 2, axis=-1)
    qh = q.reshape(b, s, XA_HEADS, XA_HEAD_DIM)
    kh = k.reshape(b, m, XA_HEADS, XA_HEAD_DIM)
    vh = v.reshape(b, m, XA_HEADS, XA_HEAD_DIM)
    scores = jnp.einsum('bshd,bmhd->bhsm', qh, kh).astype(jnp.float32) * (XA_HEAD_DIM ** -0.5)
    p = jax.nn.softmax(scores, axis=-1).astype(vh.dtype)
    return jnp.einsum('bhsm,bmhd->bshd', p, vh).reshape(b, s, XA_INNER)


def peer_ffn(xn, w_peer_q, peer_keys, peer_u, peer_v):
    b, s, d = xn.shape
    xb = xn.reshape((b * s) // PEER_TOKEN_BLOCK, PEER_TOKEN_BLOCK, d)

    def block(xt):
        t = xt.shape[0]
        q = (xt @ w_peer_q).reshape(t, PEER_HEADS, 2, PEER_HALF)
        sc = jnp.einsum('thid,hikd->thik', q, peer_keys).astype(jnp.float32)
        top_v, top_i = lax.top_k(sc, PEER_TOPK)
        cand = (top_v[:, :, 0, :, None] + top_v[:, :, 1, None, :]).reshape(t, PEER_HEADS, PEER_TOPK * PEER_TOPK)
        cidx = (top_i[:, :, 0, :, None] * PEER_N_KEYS + top_i[:, :, 1, None, :]).reshape(t, PEER_HEADS, PEER_TOPK * PEER_TOPK)
        best_v, best_pos = lax.top_k(cand, PEER_TOPK)
        eid = jnp.take_along_axis(cidx, best_pos, axis=-1)
        gate = jax.nn.softmax(best_v, axis=-1).astype(xt.dtype)
        u = peer_u[eid]
        act = jax.nn.gelu(jnp.einsum('thkd,td->thk', u, xt))
        v = peer_v[eid]
        return jnp.einsum('thk,thkd->td', gate * act, v)

    return lax.map(block, xb).reshape(b, s, d)


def setup_inputs(seed: int = 0) -> dict:
    key = jax.random.key(seed)
    ks = jax.random.split(key, 24)
    f32 = jnp.float32
    L = DEPTH

    def normal(k, shape, scale):
        return jax.random.normal(k, shape, f32) * scale

    def gain(k, shape):
        return 1.0 + 0.01 * jax.random.normal(k, shape, f32)

    x = normal(ks[0], (BATCH, SEQ, D_MODEL), 1.0)
    mem = normal(ks[1], (BATCH, MEM_LEN, D_MODEL), 1.0)
    ln_mix = gain(ks[2], (L, D_MODEL))
    w_in = normal(ks[3], (L, D_MODEL, IN_DIM), D_MODEL ** -0.5)
    conv_w = normal(ks[4], (L, SSD_CONV, SSD_CONV_DIM), SSD_CONV ** -0.5)
    conv_b = normal(ks[5], (L, SSD_CONV_DIM), 0.01)
    dt0 = jnp.exp(jax.random.uniform(ks[6], (L, SSD_HEADS), f32, math.log(DT_MIN), math.log(DT_MAX)))
    dt_bias = dt0 + jnp.log(-jnp.expm1(-dt0))
    a_log = jnp.log(jax.random.uniform(ks[7], (L, SSD_HEADS), f32, 1.0, 16.0))
    d_skip = gain(ks[8], (L, SSD_HEADS))
    ssd_norm = gain(ks[9], (L, SSD_INNER))
    gmlp_norm = gain(ks[10], (L, GMLP_INNER))
    w_spatial = normal(ks[11], (L, GMLP_GROUPS, GMLP_CHUNK, GMLP_CHUNK), GMLP_CHUNK ** -0.5)
    b_spatial = gain(ks[12], (L, GMLP_GROUPS, GMLP_CHUNK))
    ln_mem = gain(ks[13], (L, D_MODEL))
    w_mem_kv = normal(ks[14], (L, D_MODEL, 2 * XA_INNER), D_MODEL ** -0.5)
    w_branch = normal(ks[15], (L, N_BRANCH, BRANCH_WIDTH, D_MODEL), BRANCH_WIDTH ** -0.5)
    w_out = normal(ks[16], (L, D_MODEL, D_MODEL), D_MODEL ** -0.5)
    ln_ffn = gain(ks[17], (L, D_MODEL))
    w_peer_q = normal(ks[18], (L, D_MODEL, PEER_HEADS * PEER_QUERY_DIM), D_MODEL ** -0.5)
    peer_keys = normal(ks[19], (L, PEER_HEADS, 2, PEER_N_KEYS, PEER_HALF), PEER_HALF ** -0.5)
    peer_u = normal(ks[20], (L, PEER_N_EXPERTS, D_MODEL), D_MODEL ** -0.5)
    peer_v = normal(ks[21], (L, PEER_N_EXPERTS, D_MODEL), PEER_HEADS ** -0.5)
    ln_final = gain(ks[22], (D_MODEL,))
    return {'x': x, 'mem': mem, 'ln_mix': ln_mix, 'w_in': w_in, 'conv_w': conv_w, 'conv_b': conv_b,
            'dt_bias': dt_bias, 'a_log': a_log, 'd_skip': d_skip, 'ssd_norm': ssd_norm,
            'gmlp_norm': gmlp_norm, 'w_spatial': w_spatial, 'b_spatial': b_spatial,
            'ln_mem': ln_mem, 'w_mem_kv': w_mem_kv, 'w_branch': w_branch, 'w_out': w_out,
            'ln_ffn': ln_ffn, 'w_peer_q': w_peer_q, 'peer_keys': peer_keys, 'peer_u': peer_u,
            'peer_v': peer_v, 'ln_final': ln_final}


def reference(x, mem, ln_mix, w_in, conv_w, conv_b, dt_bias, a_log, d_skip, ssd_norm,
              gmlp_norm, w_spatial, b_spatial, ln_mem, w_mem_kv, w_branch, w_out,
              ln_ffn, w_peer_q, peer_keys, peer_u, peer_v, ln_final):
    b, s = x.shape[0], x.shape[1]
    bounds = np.cumsum([SSD_INNER, SSD_CONV_DIM, SSD_HEADS, GMLP_INNER, GMLP_INNER, XA_INNER]).tolist()
    for l in range(DEPTH):
        h = rms_norm(x, ln_mix[l])
        proj = h @ w_in[l]
        z, xbc, dt_raw, u, v, q, gates = jnp.split(proj, bounds, axis=-1)
        y_ssd = ssd_branch(xbc, z, dt_raw, conv_w[l], conv_b[l], dt_bias[l], a_log[l], d_skip[l], ssd_norm[l])
        y_gmlp = gmlp_branch(u, v, gmlp_norm[l], w_spatial[l], b_spatial[l])
        y_mem = memory_cross_attention(q, rms_norm(mem, ln_mem[l]), w_mem_kv[l])
        branches = jnp.stack([y_ssd, y_gmlp, y_mem], axis=2)
        proj_b = jnp.einsum('bsnw,nwd->bsnd', branches, w_branch[l])
        g = jax.nn.sigmoid(gates.reshape(b, s, N_BRANCH, D_MODEL))
        merged = jnp.sum(g * proj_b, axis=2)
        x = x + merged @ w_out[l]
        x = x + peer_ffn(rms_norm(x, ln_ffn[l]), w_peer_q[l], peer_keys[l], peer_u[l], peer_v[l])
    return rms_norm(x, ln_final)
```

```python
import jax
import jax.numpy as jnp
from jax import lax
from jax.experimental import pallas as pl
from jax.experimental.pallas import tpu as pltpu

F32 = jnp.float32
BF16 = jnp.bfloat16
I32 = jnp.int32
HI = lax.Precision.HIGHEST

D = 1024
EPS = 1e-6
CHUNK = 128
SSD_HEADS = 16
SSD_P = 64
SSD_N = 128
SSD_G = 2
CONV_K = 4
CONV_DIM = 1536
GM_G = 8
XA_H = 4
XA_D = 256
PEER_H = 8
PEER_K = 16
PEER_KEYS = 128
PEER_SEL = PEER_H * PEER_K
PEER_QD = 256

PROJ_W = 8704
COL_Z, COL_U, COL_V, COL_Q, COL_G, COL_XS = 0, 1, 2, 3, 4, 7
COL_BC = 16

VMEM_LIMIT = 52 * 1024 * 1024


def _cp(sem):
    return pltpu.CompilerParams(dimension_semantics=sem, vmem_limit_bytes=VMEM_LIMIT)


def _rms(x, g):
    return x * lax.rsqrt(jnp.mean(x * x, axis=-1, keepdims=True) + EPS) * g


def _dot(a, b):
    return jnp.dot(a, b, preferred_element_type=F32)


def _dot_nt(a, b):
    return lax.dot_general(a, b, (((1,), (1,)), ((), ())), preferred_element_type=F32)


def _dot_tn(a, b):
    return lax.dot_general(a, b, (((0,), (0,)), ((), ())), preferred_element_type=F32)


def _inproj_kernel(x_ref, g_ref, w_ref, wdt_ref, proj_ref, dt_ref, h_ref):
    j = pl.program_id(1)

    @pl.when(j == 0)
    def _():
        h = _rms(x_ref[...], g_ref[...]).astype(BF16)
        h_ref[...] = h
        dt_ref[...] = _dot(h, wdt_ref[...])

    proj_ref[...] = _dot(h_ref[...], w_ref[...])


def _inproj(x2, g, w, wdt, tm, tn):
    t = x2.shape[0]
    return pl.pallas_call(
        _inproj_kernel,
        grid=(t // tm, PROJ_W // tn),
        in_specs=[
            pl.BlockSpec((tm, D), lambda i, j: (i, 0)),
            pl.BlockSpec((1, D), lambda i, j: (0, 0)),
            pl.BlockSpec((D, tn), lambda i, j: (0, j)),
            pl.BlockSpec((D, 128), lambda i, j: (0, 0)),
        ],
        out_specs=[
            pl.BlockSpec((tm, tn), lambda i, j: (i, j)),
            pl.BlockSpec((tm, 128), lambda i, j: (i, 0)),
        ],
        out_shape=[
            jax.ShapeDtypeStruct((t, PROJ_W), F32),
            jax.ShapeDtypeStruct((t, 128), F32),
        ],
        scratch_shapes=[pltpu.VMEM((tm, D), BF16)],
        compiler_params=_cp(("parallel", "arbitrary")),
        name="in_proj",
    )(x2, g, w, wdt)


def _ssd_kernel(xs_ref, bc_ref, dt_ref, z_ref, cw_ref, cb_ref, dtb_ref, a_ref, dsk_ref,
                nrm_ref, e_ref, o_ref, xbuf, state):
    c = pl.program_id(1)
    L = CHUNK

    @pl.when(c == 0)
    def _():
        xbuf[0:8, :] = jnp.zeros((8, CONV_DIM), F32)
        state[...] = jnp.zeros_like(state)

    xbuf[8:8 + L, 0:D] = xs_ref[...]
    xbuf[8:8 + L, D:CONV_DIM] = bc_ref[...]
    w = cw_ref[...]
    acc = cb_ref[...] + w[CONV_K - 1:CONV_K, :] * xbuf[8:8 + L, :]
    for s in range(1, CONV_K):
        acc = acc + w[CONV_K - 1 - s:CONV_K - s, :] * xbuf[8 - s:8 - s + L, :]
    xbuf[0:8, :] = xbuf[L:L + 8, :]
    xbc = acc * jax.nn.sigmoid(acc)
    xs = xbc[:, 0:D]

    dt = jax.nn.softplus(dt_ref[...] + dtb_ref[...])
    adt = dt * a_ref[...]
    row = lax.broadcasted_iota(I32, (L, L), 0)
    col = lax.broadcasted_iota(I32, (L, L), 1)
    causal = row >= col
    a_cs = jnp.dot(causal.astype(F32), adt, precision=HI, preferred_element_type=F32)
    a_cs_t = a_cs.T
    e = e_ref[...]

    def expand(v):
        return jnp.dot(v, e, precision=HI, preferred_element_type=F32)

    dt_e = expand(dt)
    da_e = expand(jnp.exp(a_cs))
    dec_e = expand(jnp.exp(a_cs[L - 1:L, :] - a_cs))
    xdt = xs * dt_e
    xdt_b = xdt.astype(BF16)
    xdec_b = (xdt * dec_e).astype(BF16)
    lane_lo = lax.broadcasted_iota(I32, (L, 128), 1) < SSD_P

    hpg = SSD_HEADS // SSD_G
    gw = hpg * SSD_P
    y_diag, y_off = [], []
    for g in range(SSD_G):
        b0 = D + g * SSD_N
        c0 = D + SSD_G * SSD_N + g * SSD_N
        bm_b = xbc[:, b0:b0 + SSD_N].astype(BF16)
        cm_b = xbc[:, c0:c0 + SSD_N].astype(BF16)
        cb = _dot_nt(cm_b, bm_b)
        st = state[g]
        y_off.append(_dot(cm_b, st.astype(BF16)))
        for k in range(hpg // 2):
            pair = g * (hpg // 2) + k
            xp = xdt_b[:, pair * 128:(pair + 1) * 128]
            res = []
            for hh in range(2):
                h = pair * 2 + hh
                seg = a_cs[:, h:h + 1] - a_cs_t[h:h + 1, :]
                lmat = jnp.exp(jnp.where(causal, seg, -jnp.inf))
                res.append(_dot((cb * lmat).astype(BF16), xp))
            y_diag.append(jnp.where(lane_lo, res[0], res[1]))
        new = _dot_tn(bm_b, xdec_b[:, g * gw:(g + 1) * gw])
        state[g] = st * da_e[L - 1:L, g * gw:(g + 1) * gw] + new
    y = jnp.concatenate(y_diag, axis=1) + jnp.concatenate(y_off, axis=1) * da_e
    y = y + dsk_ref[...] * xs
    zz = z_ref[...]
    y = y * (zz * jax.nn.sigmoid(zz))
    o_ref[...] = _rms(y, nrm_ref[...]).astype(o_ref.dtype)


def _ssd(proj, dt, cw, cb, dtb, a, dsk, nrm, e, b, nc):
    t = proj.shape[0]
    full = lambda shape: pl.BlockSpec(shape, lambda i, j: (0,) * len(shape))
    return pl.pallas_call(
        _ssd_kernel,
        grid=(b, nc),
        in_specs=[
            pl.BlockSpec((CHUNK, D), lambda i, j: (i * nc + j, COL_XS)),
            pl.BlockSpec((CHUNK, 512), lambda i, j: (i * nc + j, COL_BC)),
            pl.BlockSpec((CHUNK, 128), lambda i, j: (i * nc + j, 0)),
            pl.BlockSpec((CHUNK, D), lambda i, j: (i * nc + j, COL_Z)),
            full((CONV_K, CONV_DIM)), full((1, CONV_DIM)), full((1, 128)), full((1, 128)),
            full((1, D)), full((1, D)), full((128, D)),
        ],
        out_specs=pl.BlockSpec((CHUNK, D), lambda i, j: (i * nc + j, 0)),
        out_shape=jax.ShapeDtypeStruct((t, D), BF16),
        scratch_shapes=[pltpu.VMEM((CHUNK + 8, CONV_DIM), F32),
                        pltpu.VMEM((SSD_G, SSD_N, (SSD_HEADS // SSD_G) * SSD_P), F32)],
        compiler_params=_cp(("parallel", "arbitrary")),
        name="ssd",
    )(proj, proj, dt, proj, cw, cb, dtb, a, dsk, nrm, e)


def _gmlp_kernel(u_ref, v_ref, nrm_ref, ws_ref, bs_ref, o_ref):
    L = CHUNK
    v = _rms(jax.nn.gelu(v_ref[...]), nrm_ref[...]).astype(BF16)
    row = lax.broadcasted_iota(I32, (L, L), 0)
    col = lax.broadcasted_iota(I32, (L, L), 1)
    causal = row >= col
    parts = []
    for g in range(GM_G):
        wg = jnp.where(causal, ws_ref[g], 0.0).astype(BF16)
        parts.append(_dot(wg, v[:, g * 128:(g + 1) * 128]))
    mixed = jnp.concatenate(parts, axis=1) + bs_ref[...]
    o_ref[...] = (jax.nn.gelu(u_ref[...]) * mixed).astype(o_ref.dtype)


def _gmlp(proj, nrm, ws, bs_e):
    t = proj.shape[0]
    return pl.pallas_call(
        _gmlp_kernel,
        grid=(t // CHUNK,),
        in_specs=[
            pl.BlockSpec((CHUNK, D), lambda i: (i, COL_U)),
            pl.BlockSpec((CHUNK, D), lambda i: (i, COL_V)),
            pl.BlockSpec((1, D), lambda i: (0, 0)),
            pl.BlockSpec((GM_G, CHUNK, CHUNK), lambda i: (0, 0, 0)),
            pl.BlockSpec((CHUNK, D), lambda i: (0, 0)),
        ],
        out_specs=pl.BlockSpec((CHUNK, D), lambda i: (i, 0)),
        out_shape=jax.ShapeDtypeStruct((t, D), BF16),
        compiler_params=_cp(("parallel",)),
        name="gmlp",
    )(proj, proj, nrm, ws, bs_e)


def _memkv_kernel(m_ref, g_ref, w_ref, o_ref):
    mn = _rms(m_ref[0], g_ref[...]).astype(BF16)
    o_ref[0] = _dot(mn, w_ref[...]).astype(o_ref.dtype)


def _memkv(mem, g, w):
    b, m, _ = mem.shape
    return pl.pallas_call(
        _memkv_kernel,
        grid=(b,),
        in_specs=[
            pl.BlockSpec((1, m, D), lambda i: (i, 0, 0)),
            pl.BlockSpec((1, D), lambda i: (0, 0)),
            pl.BlockSpec((D, 2 * D), lambda i: (0, 0)),
        ],
        out_specs=pl.BlockSpec((1, m, 2 * D), lambda i: (i, 0, 0)),
        out_shape=jax.ShapeDtypeStruct((b, m, 2 * D), BF16),
        compiler_params=_cp(("parallel",)),
        name="mem_kv",
    )(mem, g, w)


def _xattn_kernel(q_ref, kv_ref, o_ref):
    q = q_ref[...].astype(BF16)
    kv = kv_ref[0]
    outs = []
    for h in range(XA_H):
        qh = q[:, h * XA_D:(h + 1) * XA_D]
        kh = kv[:, h * XA_D:(h + 1) * XA_D]
        vh = kv[:, D + h * XA_D:D + (h + 1) * XA_D]
        s = _dot_nt(qh, kh) * (XA_D ** -0.5)
        s = s - jnp.max(s, axis=-1, keepdims=True)
        p = jnp.exp(s)
        p = p / jnp.sum(p, axis=-1, keepdims=True)
        outs.append(_dot(p.astype(BF16), vh))
    o_ref[...] = jnp.concatenate(outs, axis=1).astype(o_ref.dtype)


def _xattn(proj, kv, tq, s_len):
    t = proj.shape[0]
    m = kv.shape[1]
    per_b = s_len // tq
    return pl.pallas_call(
        _xattn_kernel,
        grid=(t // tq,),
        in_specs=[
            pl.BlockSpec((tq, D), lambda i: (i, COL_Q)),
            pl.BlockSpec((1, m, 2 * D), lambda i: (i // per_b, 0, 0)),
        ],
        out_specs=pl.BlockSpec((tq, D), lambda i: (i, 0)),
        out_shape=jax.ShapeDtypeStruct((t, D), BF16),
        compiler_params=_cp(("parallel",)),
        name="mem_xattn",
    )(proj, kv)


def _merge_kernel(x_ref, ys_ref, yg_ref, ym_ref, g0_ref, g1_ref, g2_ref, wb_ref, wo_ref,
                  lnf_ref, x1_ref, xn_ref):
    merged = jax.nn.sigmoid(g0_ref[...]) * _dot(ys_ref[...], wb_ref[0])
    merged = merged + jax.nn.sigmoid(g1_ref[...]) * _dot(yg_ref[...], wb_ref[1])
    merged = merged + jax.nn.sigmoid(g2_ref[...]) * _dot(ym_ref[...], wb_ref[2])
    x1 = x_ref[...] + _dot(merged.astype(BF16), wo_ref[...])
    x1_ref[...] = x1
    xn_ref[...] = _rms(x1, lnf_ref[...]).astype(xn_ref.dtype)


def _merge(x2, ys, yg, ym, proj, wb, wo, lnf, tm):
    t = x2.shape[0]
    tile = lambda cb: pl.BlockSpec((tm, D), lambda i: (i, cb))
    return pl.pallas_call(
        _merge_kernel,
        grid=(t // tm,),
        in_specs=[
            tile(0), tile(0), tile(0), tile(0),
            tile(COL_G), tile(COL_G + 1), tile(COL_G + 2),
            pl.BlockSpec((3, D, D), lambda i: (0, 0, 0)),
            pl.BlockSpec((D, D), lambda i: (0, 0)),
            pl.BlockSpec((1, D), lambda i: (0, 0)),
        ],
        out_specs=[tile(0), tile(0)],
        out_shape=[jax.ShapeDtypeStruct((t, D), F32), jax.ShapeDtypeStruct((t, D), BF16)],
        compiler_params=_cp(("parallel",)),
        name="merge",
    )(x2, ys, yg, ym, proj, proj, proj, wb, wo, lnf)


def _topk_rows(s, k, order=None, payload=None):
    if order is None:
        order = lax.broadcasted_iota(I32, s.shape, 0)
    big = jnp.int32(2 ** 30)
    vals, sel = [], []
    for _ in range(k):
        m = jnp.max(s, axis=0, keepdims=True)
        am = jnp.min(jnp.where(s == m, order, big), axis=0, keepdims=True)
        hit = order == am
        vals.append(m)
        if payload is None:
            sel.append(am)
        else:
            sel.append(jnp.max(jnp.where(hit, payload, -1), axis=0, keepdims=True))
        s = jnp.where(hit, -jnp.inf, s)
    return jnp.concatenate(vals, axis=0), jnp.concatenate(sel, axis=0)


def _pair_blocks():
    blocks = [("a", 0, 0), ("a", 0, 8)] + [("a", a, 0) for a in range(1, 8)] + [("b", 8, 0)]
    covered = set()
    for kind, p, q in blocks:
        covered |= {(p, q + r) if kind == "a" else (p + r, q) for r in range(8)}
    assert all((a, b) in covered for a in range(PEER_K) for b in range(PEER_K) if (a + 1) * (b + 1) <= PEER_K)
    return blocks


def _route_kernel(xn_ref, wq_ref, keys_ref, eid_ref, gate_ref, qt_ref, eidt_ref):
    qt_ref[...] = _dot_nt(wq_ref[...], xn_ref[...])

    def head(h, carry):
        tv, ti = [], []
        for i in range(2):
            q = qt_ref[pl.ds(pl.multiple_of(h * PEER_QD + i * 128, 128), 128), :].astype(BF16)
            sc = _dot(keys_ref[h * 2 + i], q)
            v, ix = _topk_rows(sc, PEER_K)
            tv.append(v)
            ti.append(ix)
        cand, cidx, order = [], [], []
        rows8 = lax.broadcasted_iota(I32, (8, 1), 0)
        for kind, p, q in _pair_blocks():
            if kind == "a":
                cand.append(tv[0][p:p + 1, :] + tv[1][q:q + 8, :])
                cidx.append(ti[0][p:p + 1, :] * PEER_KEYS + ti[1][q:q + 8, :])
                order.append(p * PEER_K + q + rows8)
            else:
                cand.append(tv[0][p:p + 8, :] + tv[1][q:q + 1, :])
                cidx.append(ti[0][p:p + 8, :] * PEER_KEYS + ti[1][q:q + 1, :])
                order.append((p + rows8) * PEER_K + q)
        cand = jnp.concatenate(cand, axis=0)
        order = jnp.broadcast_to(jnp.concatenate(order, axis=0), cand.shape)
        bv, be = _topk_rows(cand, PEER_K, order=order, payload=jnp.concatenate(cidx, axis=0))
        p = jnp.exp(bv - bv[0:1, :])
        gate = p / jnp.sum(p, axis=0, keepdims=True)
        r0 = pl.multiple_of(h * PEER_K, PEER_K)
        gate_ref[pl.ds(r0, PEER_K), :] = gate
        eidt_ref[pl.ds(r0, PEER_K), :] = be
        return carry

    lax.fori_loop(0, PEER_H, head, 0)
    eid_ref[...] = eidt_ref[...].T


def _route(xn, wqt, keys, tr):
    t = xn.shape[0]
    return pl.pallas_call(
        _route_kernel,
        grid=(t // tr,),
        in_specs=[
            pl.BlockSpec((tr, D), lambda i: (i, 0)),
            pl.BlockSpec((PEER_H * PEER_QD, D), lambda i: (0, 0)),
            pl.BlockSpec((PEER_H * 2, PEER_KEYS, 128), lambda i: (0, 0, 0)),
        ],
        out_specs=[
            pl.BlockSpec((tr, PEER_SEL), lambda i: (i, 0)),
            pl.BlockSpec((PEER_SEL, tr), lambda i: (0, i)),
        ],
        out_shape=[jax.ShapeDtypeStruct((t, PEER_SEL), I32),
                   jax.ShapeDtypeStruct((PEER_SEL, t), F32)],
        scratch_shapes=[pltpu.VMEM((PEER_H * PEER_QD, tr), F32),
                        pltpu.VMEM((PEER_SEL, tr), I32)],
        compiler_params=_cp(("parallel",)),
        name="peer_route",
    )(xn, wqt, keys)


PEER_TT = 128
PEER_SLOTS = 8
PEER_AHEAD = 6
PEER_RING = 3
TOK_SUB = 8
ROW_SUB = 2 * TOK_SUB


def _sum_all(v):
    return jnp.sum(jnp.sum(v, axis=1, keepdims=True), axis=0, keepdims=True)


def _peer_kernel(eid_hbm, uv_hbm, slot_hbm, x1_ref, gate_ref, lnf_ref, lno_ref, o_ref,
                 idx_smem, isem, gsem, *bufs):
    i = pl.program_id(0)
    n = pl.num_programs(0)
    tt = PEER_TT
    nidx = tt * PEER_SEL

    def idx_copy(step):
        src = pl.multiple_of(jnp.minimum(step, n - 1) * nidx, nidx)
        r = step % PEER_RING
        return pltpu.make_async_copy(eid_hbm.at[pl.ds(src, nidx)],
                                     idx_smem.at[pl.ds(pl.multiple_of(r * nidx, nidx), nidx)], isem.at[r])

    def issue(gtok, slot):
        base = (gtok % (PEER_RING * tt)) * PEER_SEL
        for j in range(PEER_SEL):
            e = idx_smem[base + j]
            pltpu.make_async_copy(uv_hbm.at[e], bufs[slot].at[:, pl.ds(j, 1), :], gsem.at[slot]).start()

    def wait(slot):
        pltpu.make_async_copy(slot_hbm, bufs[slot], gsem.at[slot]).wait()

    @pl.when(i == 0)
    def _():
        idx_copy(0).start()
        idx_copy(1).start()
        idx_copy(0).wait()
        for t in range(PEER_AHEAD):
            issue(t, t)

    idx_copy(i + 1).wait()
    idx_copy(i + 2).start()

    lane = lax.broadcasted_iota(I32, (PEER_SEL, tt), 1)

    def compute(t, slot):
        x1 = x1_ref[t]
        ms = _sum_all(x1 * x1) * (1.0 / D)
        xn = x1 * lax.rsqrt(ms + EPS) * lnf_ref[...]
        acc = jnp.zeros((PEER_SEL, 128), F32)
        for c in range(TOK_SUB):
            acc = acc + bufs[slot][c] * xn[c:c + 1, :]
        a = jnp.sum(acc, axis=1, keepdims=True)
        g = jnp.sum(jnp.where(lane == t, gate_ref[...], 0.0), axis=1, keepdims=True)
        wgt = jnp.broadcast_to(g * jax.nn.gelu(a), (PEER_SEL, 128))
        rows = []
        for c in range(TOK_SUB):
            rows.append(jnp.sum(bufs[slot][TOK_SUB + c] * wgt, axis=0, keepdims=True))
        y = x1 + jnp.concatenate(rows, axis=0)
        ms2 = _sum_all(y * y) * (1.0 / D)
        o_ref[t] = y * lax.rsqrt(ms2 + EPS) * lno_ref[...]

    def group(g, carry):
        for k in range(PEER_SLOTS):
            t = g * PEER_SLOTS + k
            issue(i * tt + t + PEER_AHEAD, (k + PEER_AHEAD) % PEER_SLOTS)
            wait(k)
            compute(t, k)
        return carry

    lax.fori_loop(0, tt // PEER_SLOTS, group, 0)

    @pl.when(i == n - 1)
    def _():
        idx_copy(i + 2).wait()
        for t in range(PEER_AHEAD):
            wait(t)


def _peer(eid_flat, uv, x1_3, gate_t, lnf3, lno3):
    t = x1_3.shape[0]
    tt = PEER_TT
    assert tt % PEER_SLOTS == 0 and PEER_AHEAD < PEER_SLOTS and t % tt == 0
    slot_shape = (ROW_SUB, PEER_SEL, 128)
    return pl.pallas_call(
        _peer_kernel,
        grid=(t // tt,),
        in_specs=[
            pl.BlockSpec(memory_space=pl.ANY),
            pl.BlockSpec(memory_space=pl.ANY),
            pl.BlockSpec(memory_space=pl.ANY),
            pl.BlockSpec((tt, TOK_SUB, 128), lambda i: (i, 0, 0)),
            pl.BlockSpec((PEER_SEL, tt), lambda i: (0, i)),
            pl.BlockSpec((TOK_SUB, 128), lambda i: (0, 0)),
            pl.BlockSpec((TOK_SUB, 128), lambda i: (0, 0)),
        ],
        out_specs=pl.BlockSpec((tt, TOK_SUB, 128), lambda i: (i, 0, 0)),
        out_shape=jax.ShapeDtypeStruct((t, TOK_SUB, 128), F32),
        scratch_shapes=[
            pltpu.SMEM((PEER_RING * tt * PEER_SEL,), I32),
            pltpu.SemaphoreType.DMA((PEER_RING,)),
            pltpu.SemaphoreType.DMA((PEER_SLOTS,)),
        ] + [pltpu.VMEM(slot_shape, F32) for _ in range(PEER_SLOTS)],
        compiler_params=_cp(("arbitrary",)),
        name="peer_experts",
    )(eid_flat, uv, jnp.zeros(slot_shape, F32), x1_3, gate_t, lnf3, lno3)


def _pad_lanes(v, n=128):
    return jnp.pad(v, (0, n - v.shape[0])).reshape(1, n)


def _layer(x2, mem, b, s, ln_mix, w_in, conv_w, conv_b, dt_bias, a_log, d_skip, ssd_norm,
           gmlp_norm, w_spatial, b_spatial, ln_mem, w_mem_kv, w_branch, w_out, ln_ffn,
           w_peer_q, peer_keys, peer_u, peer_v, ln_out):
    t = x2.shape[0]
    nc = s // CHUNK
    bnd = [0, 1024, 2560, 2576, 3600, 4624, 5648, 8720]
    wz, wxbc, wdt, wu, wv, wq, wg = (w_in[:, bnd[k]:bnd[k + 1]] for k in range(7))
    w_main = jnp.concatenate([wz, wu, wv, wq, wg, wxbc], axis=1).astype(BF16)
    w_dt = jnp.pad(wdt, ((0, 0), (0, 128 - SSD_HEADS))).astype(BF16)
    tm = min(512, t)
    proj, dt_raw = _inproj(x2, ln_mix.reshape(1, D), w_main, w_dt, tm, 2176)

    a_neg = _pad_lanes(-jnp.exp(a_log.astype(F32)))
    expand = (jnp.arange(128)[:, None] == (jnp.arange(D)[None, :] // SSD_P)).astype(F32)
    y_ssd = _ssd(proj, dt_raw, conv_w, conv_b.reshape(1, CONV_DIM), _pad_lanes(dt_bias.astype(F32)),
                 a_neg, jnp.repeat(d_skip, SSD_P).reshape(1, D), ssd_norm.reshape(1, D), expand, b, nc)

    bs_e = jnp.repeat(b_spatial.T, D // GM_G, axis=1)
    y_gmlp = _gmlp(proj, gmlp_norm.reshape(1, D), w_spatial, bs_e)

    kv = _memkv(mem, ln_mem.reshape(1, D), w_mem_kv.astype(BF16))
    y_mem = _xattn(proj, kv, min(512, s), s)

    x1, xn = _merge(x2, y_ssd, y_gmlp, y_mem, proj, w_branch.astype(BF16), w_out.astype(BF16),
                    ln_ffn.reshape(1, D), min(256, t))

    keys = peer_keys.reshape(PEER_H * 2, PEER_KEYS, 128).astype(BF16)
    eid, gate_t = _route(xn, w_peer_q.T.astype(BF16), keys, min(256, t))

    uv = jnp.concatenate([peer_u, peer_v], axis=1).reshape(-1, ROW_SUB, 1, 128)
    out3 = _peer(eid.reshape(-1), uv, x1.reshape(t, TOK_SUB, 128), gate_t,
                 ln_ffn.reshape(TOK_SUB, 128), ln_out.reshape(TOK_SUB, 128))
    return out3.reshape(t, D)


def kernel(x, mem, ln_mix, w_in, conv_w, conv_b, dt_bias, a_log, d_skip, ssd_norm, gmlp_norm,
           w_spatial, b_spatial, ln_mem, w_mem_kv, w_branch, w_out, ln_ffn, w_peer_q, peer_keys,
           peer_u, peer_v, ln_final):
    b, s, _ = x.shape
    depth = ln_mix.shape[0]
    assert depth == 1, "the PEER kernel fuses the final norm, so it supports a single layer"
    out = _layer(x.reshape(b * s, D), mem, b, s, ln_mix[0], w_in[0], conv_w[0], conv_b[0],
                 dt_bias[0], a_log[0], d_skip[0], ssd_norm[0], gmlp_norm[0], w_spatial[0],
                 b_spatial[0], ln_mem[0], w_mem_kv[0], w_branch[0], w_out[0], ln_ffn[0],
                 w_peer_q[0], peer_keys[0], peer_u[0], peer_v[0], ln_final)
    return out.reshape(b, s, D)
```

```python
import jax
import jax.numpy as jnp
from jax import lax
from jax.experimental import pallas as pl
from jax.experimental.pallas import tpu as pltpu

F32 = jnp.float32
BF16 = jnp.bfloat16
I32 = jnp.int32
HI = lax.Precision.HIGHEST

D = 1024
EPS = 1e-6
CHUNK = 128
SSD_HEADS = 16
SSD_P = 64
SSD_N = 128
SSD_G = 2
CONV_K = 4
CONV_DIM = 1536
GM_G = 8
XA_H = 4
XA_D = 256
PEER_H = 8
PEER_K = 16
PEER_KEYS = 128
PEER_SEL = PEER_H * PEER_K
PEER_QD = 256

PROJ_W = 8704
COL_Z, COL_U, COL_V, COL_Q, COL_G, COL_XS = 0, 1, 2, 3, 4, 7
COL_BC = 16

VMEM_LIMIT = 52 * 1024 * 1024


def _cp(sem):
    return pltpu.CompilerParams(dimension_semantics=sem, vmem_limit_bytes=VMEM_LIMIT)


def _rms(x, g):
    return x * lax.rsqrt(jnp.mean(x * x, axis=-1, keepdims=True) + EPS) * g


def _dot(a, b):
    return jnp.dot(a, b, preferred_element_type=F32)


def _dot_nt(a, b):
    return lax.dot_general(a, b, (((1,), (1,)), ((), ())), preferred_element_type=F32)


def _dot_tn(a, b):
    return lax.dot_general(a, b, (((0,), (0,)), ((), ())), preferred_element_type=F32)


def _inproj_kernel(x_ref, g_ref, w_ref, wdt_ref, proj_ref, dt_ref, h_ref):
    j = pl.program_id(1)

    @pl.when(j == 0)
    def _():
        h = _rms(x_ref[...], g_ref[...]).astype(BF16)
        h_ref[...] = h
        dt_ref[...] = _dot(h, wdt_ref[...])

    proj_ref[...] = _dot(h_ref[...], w_ref[...])


def _inproj(x2, g, w, wdt, tm, tn):
    t = x2.shape[0]
    return pl.pallas_call(
        _inproj_kernel,
        grid=(t // tm, PROJ_W // tn),
        in_specs=[
            pl.BlockSpec((tm, D), lambda i, j: (i, 0)),
            pl.BlockSpec((1, D), lambda i, j: (0, 0)),
            pl.BlockSpec((D, tn), lambda i, j: (0, j)),
            pl.BlockSpec((D, 128), lambda i, j: (0, 0)),
        ],
        out_specs=[
            pl.BlockSpec((tm, tn), lambda i, j: (i, j)),
            pl.BlockSpec((tm, 128), lambda i, j: (i, 0)),
        ],
        out_shape=[
            jax.ShapeDtypeStruct((t, PROJ_W), F32),
            jax.ShapeDtypeStruct((t, 128), F32),
        ],
        scratch_shapes=[pltpu.VMEM((tm, D), BF16)],
        compiler_params=_cp(("parallel", "arbitrary")),
        name="in_proj",
    )(x2, g, w, wdt)


def _ssd_kernel(xs_ref, bc_ref, dt_ref, z_ref, cw_ref, cb_ref, dtb_ref, a_ref, dsk_ref,
                nrm_ref, e_ref, o_ref, xbuf, state):
    c = pl.program_id(1)
    L = CHUNK

    @pl.when(c == 0)
    def _():
        xbuf[0:8, :] = jnp.zeros((8, CONV_DIM), F32)
        state[...] = jnp.zeros_like(state)

    xbuf[8:8 + L, 0:D] = xs_ref[...]
    xbuf[8:8 + L, D:CONV_DIM] = bc_ref[...]
    w = cw_ref[...]
    acc = cb_ref[...] + w[CONV_K - 1:CONV_K, :] * xbuf[8:8 + L, :]
    for s in range(1, CONV_K):
        acc = acc + w[CONV_K - 1 - s:CONV_K - s, :] * xbuf[8 - s:8 - s + L, :]
    xbuf[0:8, :] = xbuf[L:L + 8, :]
    xbc = acc * jax.nn.sigmoid(acc)
    xs = xbc[:, 0:D]

    dt = jax.nn.softplus(dt_ref[...] + dtb_ref[...])
    adt = dt * a_ref[...]
    row = lax.broadcasted_iota(I32, (L, L), 0)
    col = lax.broadcasted_iota(I32, (L, L), 1)
    causal = row >= col
    a_cs = jnp.dot(causal.astype(F32), adt, precision=HI, preferred_element_type=F32)
    a_cs_t = a_cs.T
    e = e_ref[...]

    def expand(v):
        return jnp.dot(v, e, precision=HI, preferred_element_type=F32)

    dt_e = expand(dt)
    da_e = expand(jnp.exp(a_cs))
    dec_e = expand(jnp.exp(a_cs[L - 1:L, :] - a_cs))
    xdt = xs * dt_e
    xdt_b = xdt.astype(BF16)
    xdec_b = (xdt * dec_e).astype(BF16)
    lane_lo = lax.broadcasted_iota(I32, (L, 128), 1) < SSD_P

    hpg = SSD_HEADS // SSD_G
    gw = hpg * SSD_P
    y_diag, y_off = [], []
    for g in range(SSD_G):
        b0 = D + g * SSD_N
        c0 = D + SSD_G * SSD_N + g * SSD_N
        bm_b = xbc[:, b0:b0 + SSD_N].astype(BF16)
        cm_b = xbc[:, c0:c0 + SSD_N].astype(BF16)
        cb = _dot_nt(cm_b, bm_b)
        st = state[g]
        y_off.append(_dot(cm_b, st.astype(BF16)))
        for k in range(hpg // 2):
            pair = g * (hpg // 2) + k
            xp = xdt_b[:, pair * 128:(pair + 1) * 128]
            res = []
            for hh in range(2):
                h = pair * 2 + hh
                seg = a_cs[:, h:h + 1] - a_cs_t[h:h + 1, :]
                lmat = jnp.exp(jnp.where(causal, seg, -jnp.inf))
                res.append(_dot((cb * lmat).astype(BF16), xp))
            y_diag.append(jnp.where(lane_lo, res[0], res[1]))
        new = _dot_tn(bm_b, xdec_b[:, g * gw:(g + 1) * gw])
        state[g] = st * da_e[L - 1:L, g * gw:(g + 1) * gw] + new
    y = jnp.concatenate(y_diag, axis=1) + jnp.concatenate(y_off, axis=1) * da_e
    y = y + dsk_ref[...] * xs
    zz = z_ref[...]
    y = y * (zz * jax.nn.sigmoid(zz))
    o_ref[...] = _rms(y, nrm_ref[...]).astype(o_ref.dtype)


def _ssd(proj, dt, cw, cb, dtb, a, dsk, nrm, e, b, nc):
    t = proj.shape[0]
    full = lambda shape: pl.BlockSpec(shape, lambda i, j: (0,) * len(shape))
    return pl.pallas_call(
        _ssd_kernel,
        grid=(b, nc),
        in_specs=[
            pl.BlockSpec((CHUNK, D), lambda i, j: (i * nc + j, COL_XS)),
            pl.BlockSpec((CHUNK, 512), lambda i, j: (i * nc + j, COL_BC)),
            pl.BlockSpec((CHUNK, 128), lambda i, j: (i * nc + j, 0)),
            pl.BlockSpec((CHUNK, D), lambda i, j: (i * nc + j, COL_Z)),
            full((CONV_K, CONV_DIM)), full((1, CONV_DIM)), full((1, 128)), full((1, 128)),
            full((1, D)), full((1, D)), full((128, D)),
        ],
        out_specs=pl.BlockSpec((CHUNK, D), lambda i, j: (i * nc + j, 0)),
        out_shape=jax.ShapeDtypeStruct((t, D), BF16),
        scratch_shapes=[pltpu.VMEM((CHUNK + 8, CONV_DIM), F32),
                        pltpu.VMEM((SSD_G, SSD_N, (SSD_HEADS // SSD_G) * SSD_P), F32)],
        compiler_params=_cp(("parallel", "arbitrary")),
        name="ssd",
    )(proj, proj, dt, proj, cw, cb, dtb, a, dsk, nrm, e)


def _gmlp_kernel(u_ref, v_ref, nrm_ref, ws_ref, bs_ref, o_ref):
    L = CHUNK
    v = _rms(jax.nn.gelu(v_ref[...]), nrm_ref[...]).astype(BF16)
    row = lax.broadcasted_iota(I32, (L, L), 0)
    col = lax.broadcasted_iota(I32, (L, L), 1)
    causal = row >= col
    parts = []
    for g in range(GM_G):
        wg = jnp.where(causal, ws_ref[g], 0.0).astype(BF16)
        parts.append(_dot(wg, v[:, g * 128:(g + 1) * 128]))
    mixed = jnp.concatenate(parts, axis=1) + bs_ref[...]
    o_ref[...] = (jax.nn.gelu(u_ref[...]) * mixed).astype(o_ref.dtype)


def _gmlp(proj, nrm, ws, bs_e):
    t = proj.shape[0]
    return pl.pallas_call(
        _gmlp_kernel,
        grid=(t // CHUNK,),
        in_specs=[
            pl.BlockSpec((CHUNK, D), lambda i: (i, COL_U)),
            pl.BlockSpec((CHUNK, D), lambda i: (i, COL_V)),
            pl.BlockSpec((1, D), lambda i: (0, 0)),
            pl.BlockSpec((GM_G, CHUNK, CHUNK), lambda i: (0, 0, 0)),
            pl.BlockSpec((CHUNK, D), lambda i: (0, 0)),
        ],
        out_specs=pl.BlockSpec((CHUNK, D), lambda i: (i, 0)),
        out_shape=jax.ShapeDtypeStruct((t, D), BF16),
        compiler_params=_cp(("parallel",)),
        name="gmlp",
    )(proj, proj, nrm, ws, bs_e)


def _memkv_kernel(m_ref, g_ref, w_ref, o_ref):
    mn = _rms(m_ref[0], g_ref[...]).astype(BF16)
    o_ref[0] = _dot(mn, w_ref[...]).astype(o_ref.dtype)


def _memkv(mem, g, w):
    b, m, _ = mem.shape
    return pl.pallas_call(
        _memkv_kernel,
        grid=(b,),
        in_specs=[
            pl.BlockSpec((1, m, D), lambda i: (i, 0, 0)),
            pl.BlockSpec((1, D), lambda i: (0, 0)),
            pl.BlockSpec((D, 2 * D), lambda i: (0, 0)),
        ],
        out_specs=pl.BlockSpec((1, m, 2 * D), lambda i: (i, 0, 0)),
        out_shape=jax.ShapeDtypeStruct((b, m, 2 * D), BF16),
        compiler_params=_cp(("parallel",)),
        name="mem_kv",
    )(mem, g, w)


def _xattn_kernel(q_ref, kv_ref, o_ref):
    q = q_ref[...].astype(BF16)
    kv = kv_ref[0]
    outs = []
    for h in range(XA_H):
        qh = q[:, h * XA_D:(h + 1) * XA_D]
        kh = kv[:, h * XA_D:(h + 1) * XA_D]
        vh = kv[:, D + h * XA_D:D + (h + 1) * XA_D]
        s = _dot_nt(qh, kh) * (XA_D ** -0.5)
        s = s - jnp.max(s, axis=-1, keepdims=True)
        p = jnp.exp(s)
        p = p / jnp.sum(p, axis=-1, keepdims=True)
        outs.append(_dot(p.astype(BF16), vh))
    o_ref[...] = jnp.concatenate(outs, axis=1).astype(o_ref.dtype)


def _xattn(proj, kv, tq, s_len):
    t = proj.shape[0]
    m = kv.shape[1]
    per_b = s_len // tq
    return pl.pallas_call(
        _xattn_kernel,
        grid=(t // tq,),
        in_specs=[
            pl.BlockSpec((tq, D), lambda i: (i, COL_Q)),
            pl.BlockSpec((1, m, 2 * D), lambda i: (i // per_b, 0, 0)),
        ],
        out_specs=pl.BlockSpec((tq, D), lambda i: (i, 0)),
        out_shape=jax.ShapeDtypeStruct((t, D), BF16),
        compiler_params=_cp(("parallel",)),
        name="mem_xattn",
    )(proj, kv)


def _merge_kernel(x_ref, ys_ref, yg_ref, ym_ref, g0_ref, g1_ref, g2_ref, wb_ref, wo_ref,
                  lnf_ref, x1_ref, xn_ref):
    merged = jax.nn.sigmoid(g0_ref[...]) * _dot(ys_ref[...], wb_ref[0])
    merged = merged + jax.nn.sigmoid(g1_ref[...]) * _dot(yg_ref[...], wb_ref[1])
    merged = merged + jax.nn.sigmoid(g2_ref[...]) * _dot(ym_ref[...], wb_ref[2])
    x1 = x_ref[...] + _dot(merged.astype(BF16), wo_ref[...])
    x1_ref[...] = x1
    xn_ref[...] = _rms(x1, lnf_ref[...]).astype(xn_ref.dtype)


def _merge(x2, ys, yg, ym, proj, wb, wo, lnf, tm):
    t = x2.shape[0]
    tile = lambda cb: pl.BlockSpec((tm, D), lambda i: (i, cb))
    return pl.pallas_call(
        _merge_kernel,
        grid=(t // tm,),
        in_specs=[
            tile(0), tile(0), tile(0), tile(0),
            tile(COL_G), tile(COL_G + 1), tile(COL_G + 2),
            pl.BlockSpec((3, D, D), lambda i: (0, 0, 0)),
            pl.BlockSpec((D, D), lambda i: (0, 0)),
            pl.BlockSpec((1, D), lambda i: (0, 0)),
        ],
        out_specs=[tile(0), tile(0)],
        out_shape=[jax.ShapeDtypeStruct((t, D), F32), jax.ShapeDtypeStruct((t, D), BF16)],
        compiler_params=_cp(("parallel",)),
        name="merge",
    )(x2, ys, yg, ym, proj, proj, proj, wb, wo, lnf)


def _topk_rows(s, k, order=None, payload=None):
    if order is None:
        order = lax.broadcasted_iota(I32, s.shape, 0)
    big = jnp.int32(2 ** 30)
    vals, sel = [], []
    for _ in range(k):
        m = jnp.max(s, axis=0, keepdims=True)
        am = jnp.min(jnp.where(s == m, order, big), axis=0, keepdims=True)
        hit = order == am
        vals.append(m)
        if payload is None:
            sel.append(am)
        else:
            sel.append(jnp.max(jnp.where(hit, payload, -1), axis=0, keepdims=True))
        s = jnp.where(hit, -jnp.inf, s)
    return jnp.concatenate(vals, axis=0), jnp.concatenate(sel, axis=0)


def _pair_blocks():
    blocks = [("a", 0, 0), ("a", 0, 8)] + [("a", a, 0) for a in range(1, 8)] + [("b", 8, 0)]
    covered = set()
    for kind, p, q in blocks:
        covered |= {(p, q + r) if kind == "a" else (p + r, q) for r in range(8)}
    assert all((a, b) in covered for a in range(PEER_K) for b in range(PEER_K) if (a + 1) * (b + 1) <= PEER_K)
    return blocks


def _route_kernel(xn_ref, wq_ref, keys_ref, eid_ref, gate_ref, qt_ref, eidt_ref):
    qt_ref[...] = _dot_nt(wq_ref[...], xn_ref[...])

    def head(h, carry):
        tv, ti = [], []
        for i in range(2):
            q = qt_ref[pl.ds(pl.multiple_of(h * PEER_QD + i * 128, 128), 128), :].astype(BF16)
            sc = _dot(keys_ref[h * 2 + i], q)
            v, ix = _topk_rows(sc, PEER_K)
            tv.append(v)
            ti.append(ix)
        cand, cidx, order = [], [], []
        rows8 = lax.broadcasted_iota(I32, (8, 1), 0)
        for kind, p, q in _pair_blocks():
            if kind == "a":
                cand.append(tv[0][p:p + 1, :] + tv[1][q:q + 8, :])
                cidx.append(ti[0][p:p + 1, :] * PEER_KEYS + ti[1][q:q + 8, :])
                order.append(p * PEER_K + q + rows8)
            else:
                cand.append(tv[0][p:p + 8, :] + tv[1][q:q + 1, :])
                cidx.append(ti[0][p:p + 8, :] * PEER_KEYS + ti[1][q:q + 1, :])
                order.append((p + rows8) * PEER_K + q)
        cand = jnp.concatenate(cand, axis=0)
        order = jnp.broadcast_to(jnp.concatenate(order, axis=0), cand.shape)
        bv, be = _topk_rows(cand, PEER_K, order=order, payload=jnp.concatenate(cidx, axis=0))
        p = jnp.exp(bv - bv[0:1, :])
        gate = p / jnp.sum(p, axis=0, keepdims=True)
        r0 = pl.multiple_of(h * PEER_K, PEER_K)
        gate_ref[pl.ds(r0, PEER_K), :] = gate
        eidt_ref[pl.ds(r0, PEER_K), :] = be
        return carry

    lax.fori_loop(0, PEER_H, head, 0)
    eid_ref[...] = eidt_ref[...].T


def _route(xn, wqt, keys, tr):
    t = xn.shape[0]
    return pl.pallas_call(
        _route_kernel,
        grid=(t // tr,),
        in_specs=[
            pl.BlockSpec((tr, D), lambda i: (i, 0)),
            pl.BlockSpec((PEER_H * PEER_QD, D), lambda i: (0, 0)),
            pl.BlockSpec((PEER_H * 2, PEER_KEYS, 128), lambda i: (0, 0, 0)),
        ],
        out_specs=[
            pl.BlockSpec((tr, PEER_SEL), lambda i: (i, 0)),
            pl.BlockSpec((PEER_SEL, tr), lambda i: (0, i)),
        ],
        out_shape=[jax.ShapeDtypeStruct((t, PEER_SEL), I32),
                   jax.ShapeDtypeStruct((PEER_SEL, t), F32)],
        scratch_shapes=[pltpu.VMEM((PEER_H * PEER_QD, tr), F32),
                        pltpu.VMEM((PEER_SEL, tr), I32)],
        compiler_params=_cp(("parallel",)),
        name="peer_route",
    )(xn, wqt, keys)


PEER_TT = 128
PEER_SLOTS = 8
PEER_AHEAD = 6
PEER_RING = 3
TOK_SUB = 8
ROW_SUB = 2 * TOK_SUB


def _peer_kernel(eid_hbm, uv_hbm, slot_hbm, x1_ref, gate_ref, lnf_ref, lno_ref, o_ref,
                 idx_smem, isem, gsem, xn_ref, y_ref, *bufs):
    i = pl.program_id(0)
    n = pl.num_programs(0)
    tt = PEER_TT
    nidx = tt * PEER_SEL

    def idx_copy(step):
        src = pl.multiple_of(jnp.minimum(step, n - 1) * nidx, nidx)
        r = step % PEER_RING
        return pltpu.make_async_copy(eid_hbm.at[pl.ds(src, nidx)],
                                     idx_smem.at[pl.ds(pl.multiple_of(r * nidx, nidx), nidx)], isem.at[r])

    per_part = PEER_SEL // TOK_SUB

    def issue(gtok, slot, part=None):
        base = (gtok % (PEER_RING * tt)) * PEER_SEL
        js = range(PEER_SEL) if part is None else range(part * per_part, (part + 1) * per_part)
        for j in js:
            e = idx_smem[base + j]
            pltpu.make_async_copy(uv_hbm.at[e], bufs[slot].at[:, j, :], gsem.at[slot]).start(priority=j % 2)

    def wait(slot):
        pltpu.make_async_copy(slot_hbm, bufs[slot], gsem.at[slot]).wait()

    @pl.when(i == 0)
    def _():
        idx_copy(0).start()
        idx_copy(1).start()
        idx_copy(0).wait()
        for t in range(PEER_AHEAD):
            issue(t, t)

    idx_copy(i + 1).wait()
    idx_copy(i + 2).start()

    def tile_rms(v, g):
        ms = jnp.sum(jnp.sum(v * v, axis=2, keepdims=True), axis=1, keepdims=True) * (1.0 / D)
        return v * lax.rsqrt(ms + EPS) * g

    xn_ref[...] = tile_rms(x1_ref[...], lnf_ref[...])
    lane = lax.broadcasted_iota(I32, (PEER_SEL, tt), 1)

    def gate_col(t):
        return jnp.sum(jnp.where(lane == t, gate_ref[...], 0.0), axis=1, keepdims=True)

    def value_row(slot, c, wb):
        return jnp.sum(bufs[slot][TOK_SUB + c] * wb, axis=0, keepdims=True)

    def mix_values(t, slot, wgt):
        wb = jnp.broadcast_to(wgt, (PEER_SEL, 128))
        rows = [value_row(slot, c, wb) for c in range(TOK_SUB)]
        y_ref[t] = x1_ref[t] + jnp.concatenate(rows, axis=0)

    def group(g, carry):
        wgt_prev, gcol = carry
        for k in range(PEER_SLOTS):
            t = g * PEER_SLOTS + k
            wait(k)
            nxt, nxt_slot = i * tt + t + PEER_AHEAD, (k + PEER_AHEAD) % PEER_SLOTS
            xn = xn_ref[t]
            wb = jnp.broadcast_to(wgt_prev, (PEER_SEL, 128))
            acc = jnp.zeros((PEER_SEL, 128), F32)
            rows = []
            for c in range(TOK_SUB):
                issue(nxt, nxt_slot, c)
                acc = acc + bufs[k][c] * xn[c:c + 1, :]
                if k > 0:
                    rows.append(value_row(k - 1, c, wb))
            a = jnp.sum(acc, axis=1, keepdims=True)
            wgt = gcol * jax.nn.gelu(a)
            gcol = gate_col(t + 1)
            if k > 0:
                y_ref[t - 1] = x1_ref[t - 1] + jnp.concatenate(rows, axis=0)
            else:
                @pl.when(g > 0)
                def _():
                    mix_values(t - 1, PEER_SLOTS - 1, wgt_prev)
            wgt_prev = wgt
        return wgt_prev, gcol

    wgt_last, _ = lax.fori_loop(0, tt // PEER_SLOTS, group,
                                (jnp.zeros((PEER_SEL, 1), F32), gate_col(0)))
    mix_values(tt - 1, PEER_SLOTS - 1, wgt_last)
    o_ref[...] = tile_rms(y_ref[...], lno_ref[...])

    @pl.when(i == n - 1)
    def _():
        idx_copy(i + 2).wait()
        for t in range(PEER_AHEAD):
            wait(t)


def _peer(eid_flat, uv, x1_3, gate_t, lnf3, lno3):
    t = x1_3.shape[0]
    tt = PEER_TT
    assert tt % PEER_SLOTS == 0 and PEER_AHEAD < PEER_SLOTS and t % tt == 0
    slot_shape = (ROW_SUB, PEER_SEL, 128)
    return pl.pallas_call(
        _peer_kernel,
        grid=(t // tt,),
        in_specs=[
            pl.BlockSpec(memory_space=pl.ANY),
            pl.BlockSpec(memory_space=pl.ANY),
            pl.BlockSpec(memory_space=pl.ANY),
            pl.BlockSpec((tt, TOK_SUB, 128), lambda i: (i, 0, 0)),
            pl.BlockSpec((PEER_SEL, tt), lambda i: (0, i)),
            pl.BlockSpec((TOK_SUB, 128), lambda i: (0, 0)),
            pl.BlockSpec((TOK_SUB, 128), lambda i: (0, 0)),
        ],
        out_specs=pl.BlockSpec((tt, TOK_SUB, 128), lambda i: (i, 0, 0)),
        out_shape=jax.ShapeDtypeStruct((t, TOK_SUB, 128), F32),
        scratch_shapes=[
            pltpu.SMEM((PEER_RING * tt * PEER_SEL,), I32),
            pltpu.SemaphoreType.DMA((PEER_RING,)),
            pltpu.SemaphoreType.DMA((PEER_SLOTS,)),
            pltpu.VMEM((tt, TOK_SUB, 128), F32),
            pltpu.VMEM((tt, TOK_SUB, 128), F32),
        ] + [pltpu.VMEM(slot_shape, F32) for _ in range(PEER_SLOTS)],
        compiler_params=_cp(("arbitrary",)),
        name="peer_experts",
    )(eid_flat, uv, jnp.zeros(slot_shape, F32), x1_3, gate_t, lnf3, lno3)


def _pad_lanes(v, n=128):
    return jnp.pad(v, (0, n - v.shape[0])).reshape(1, n)


def _layer(x2, mem, b, s, ln_mix, w_in, conv_w, conv_b, dt_bias, a_log, d_skip, ssd_norm,
           gmlp_norm, w_spatial, b_spatial, ln_mem, w_mem_kv, w_branch, w_out, ln_ffn,
           w_peer_q, peer_keys, peer_u, peer_v, ln_out):
    t = x2.shape[0]
    nc = s // CHUNK
    bnd = [0, 1024, 2560, 2576, 3600, 4624, 5648, 8720]
    wz, wxbc, wdt, wu, wv, wq, wg = (w_in[:, bnd[k]:bnd[k + 1]] for k in range(7))
    w_main = jnp.concatenate([wz, wu, wv, wq, wg, wxbc], axis=1).astype(BF16)
    w_dt = jnp.pad(wdt, ((0, 0), (0, 128 - SSD_HEADS))).astype(BF16)
    tm = min(1024, t)
    proj, dt_raw = _inproj(x2, ln_mix.reshape(1, D), w_main, w_dt, tm, 2176)

    a_neg = _pad_lanes(-jnp.exp(a_log.astype(F32)))
    expand = (jnp.arange(128)[:, None] == (jnp.arange(D)[None, :] // SSD_P)).astype(F32)
    y_ssd = _ssd(proj, dt_raw, conv_w, conv_b.reshape(1, CONV_DIM), _pad_lanes(dt_bias.astype(F32)),
                 a_neg, jnp.repeat(d_skip, SSD_P).reshape(1, D), ssd_norm.reshape(1, D), expand, b, nc)

    bs_e = jnp.repeat(b_spatial.T, D // GM_G, axis=1)
    y_gmlp = _gmlp(proj, gmlp_norm.reshape(1, D), w_spatial, bs_e)

    kv = _memkv(mem, ln_mem.reshape(1, D), w_mem_kv.astype(BF16))
    y_mem = _xattn(proj, kv, min(512, s), s)

    x1, xn = _merge(x2, y_ssd, y_gmlp, y_mem, proj, w_branch.astype(BF16), w_out.astype(BF16),
                    ln_ffn.reshape(1, D), min(256, t))

    keys = peer_keys.reshape(PEER_H * 2, PEER_KEYS, 128).astype(BF16)
    eid, gate_t = _route(xn, w_peer_q.T.astype(BF16), keys, min(256, t))

    n_exp = peer_u.shape[0]
    uv = jnp.concatenate([peer_u.reshape(n_exp, TOK_SUB, 128), peer_v.reshape(n_exp, TOK_SUB, 128)],
                         axis=1)
    out3 = _peer(eid.reshape(-1), uv, x1.reshape(t, TOK_SUB, 128), gate_t,
                 ln_ffn.reshape(TOK_SUB, 128), ln_out.reshape(TOK_SUB, 128))
    return out3.reshape(t, D)


def kernel(x, mem, ln_mix, w_in, conv_w, conv_b, dt_bias, a_log, d_skip, ssd_norm, gmlp_norm,
           w_spatial, b_spatial, ln_mem, w_mem_kv, w_branch, w_out, ln_ffn, w_peer_q, peer_keys,
           peer_u, peer_v, ln_final):
    b, s, _ = x.shape
    depth = ln_mix.shape[0]
    assert depth == 1, "the PEER kernel fuses the final norm, so it supports a single layer"
    out = _layer(x.reshape(b * s, D), mem, b, s, ln_mix[0], w_in[0], conv_w[0], conv_b[0],
                 dt_bias[0], a_log[0], d_skip[0], ssd_norm[0], gmlp_norm[0], w_spatial[0],
                 b_spatial[0], ln_mem[0], w_mem_kv[0], w_branch[0], w_out[0], ln_ffn[0],
                 w_peer_q[0], peer_keys[0], peer_u[0], peer_v[0], ln_final)
    return out.reshape(b, s, D)
```

```python
import jax
import jax.numpy as jnp
from jax import lax
from jax.experimental import pallas as pl
from jax.experimental.pallas import tpu as pltpu

F32 = jnp.float32
BF16 = jnp.bfloat16
I32 = jnp.int32
HI = lax.Precision.HIGHEST

D = 1024
EPS = 1e-6
CHUNK = 128
SSD_HEADS = 16
SSD_P = 64
SSD_N = 128
SSD_G = 2
CONV_K = 4
CONV_DIM = 1536
GM_G = 8
XA_H = 4
XA_D = 256
PEER_H = 8
PEER_K = 16
PEER_KEYS = 128
PEER_SEL = PEER_H * PEER_K
PEER_QD = 256

PROJ_W = 8704
COL_Z, COL_U, COL_V, COL_Q, COL_G, COL_XS = 0, 1, 2, 3, 4, 7
COL_BC = 16

VMEM_LIMIT = 52 * 1024 * 1024


def _cp(sem):
    return pltpu.CompilerParams(dimension_semantics=sem, vmem_limit_bytes=VMEM_LIMIT)


def _rms(x, g):
    return x * lax.rsqrt(jnp.mean(x * x, axis=-1, keepdims=True) + EPS) * g


def _dot(a, b):
    return jnp.dot(a, b, preferred_element_type=F32)


def _dot_nt(a, b):
    return lax.dot_general(a, b, (((1,), (1,)), ((), ())), preferred_element_type=F32)


def _dot_tn(a, b):
    return lax.dot_general(a, b, (((0,), (0,)), ((), ())), preferred_element_type=F32)


def _inproj_kernel(x_ref, g_ref, w_ref, wdt_ref, proj_ref, dt_ref, h_ref):
    j = pl.program_id(1)

    @pl.when(j == 0)
    def _():
        h = _rms(x_ref[...], g_ref[...]).astype(BF16)
        h_ref[...] = h
        dt_ref[...] = _dot(h, wdt_ref[...])

    proj_ref[...] = _dot(h_ref[...], w_ref[...])


def _inproj(x2, g, w, wdt, tm, tn):
    t = x2.shape[0]
    return pl.pallas_call(
        _inproj_kernel,
        grid=(t // tm, PROJ_W // tn),
        in_specs=[
            pl.BlockSpec((tm, D), lambda i, j: (i, 0)),
            pl.BlockSpec((1, D), lambda i, j: (0, 0)),
            pl.BlockSpec((D, tn), lambda i, j: (0, j)),
            pl.BlockSpec((D, 128), lambda i, j: (0, 0)),
        ],
        out_specs=[
            pl.BlockSpec((tm, tn), lambda i, j: (i, j)),
            pl.BlockSpec((tm, 128), lambda i, j: (i, 0)),
        ],
        out_shape=[
            jax.ShapeDtypeStruct((t, PROJ_W), F32),
            jax.ShapeDtypeStruct((t, 128), F32),
        ],
        scratch_shapes=[pltpu.VMEM((tm, D), BF16)],
        compiler_params=_cp(("parallel", "arbitrary")),
        name="in_proj",
    )(x2, g, w, wdt)


def _ssd_kernel(xs_ref, bc_ref, dt_ref, z_ref, cw_ref, cb_ref, dtb_ref, a_ref, dsk_ref,
                nrm_ref, e_ref, o_ref, xbuf, state):
    c = pl.program_id(1)
    L = CHUNK

    @pl.when(c == 0)
    def _():
        xbuf[0:8, :] = jnp.zeros((8, CONV_DIM), F32)
        state[...] = jnp.zeros_like(state)

    xbuf[8:8 + L, 0:D] = xs_ref[...]
    xbuf[8:8 + L, D:CONV_DIM] = bc_ref[...]
    w = cw_ref[...]
    acc = cb_ref[...] + w[CONV_K - 1:CONV_K, :] * xbuf[8:8 + L, :]
    for s in range(1, CONV_K):
        acc = acc + w[CONV_K - 1 - s:CONV_K - s, :] * xbuf[8 - s:8 - s + L, :]
    xbuf[0:8, :] = xbuf[L:L + 8, :]
    xbc = acc * jax.nn.sigmoid(acc)
    xs = xbc[:, 0:D]

    dt = jax.nn.softplus(dt_ref[...] + dtb_ref[...])
    adt = dt * a_ref[...]
    row = lax.broadcasted_iota(I32, (L, L), 0)
    col = lax.broadcasted_iota(I32, (L, L), 1)
    causal = row >= col
    a_cs = jnp.dot(causal.astype(F32), adt, precision=HI, preferred_element_type=F32)
    a_cs_t = a_cs.T
    e = e_ref[...]

    def expand(v):
        return jnp.dot(v, e, precision=HI, preferred_element_type=F32)

    dt_e = expand(dt)
    da_e = expand(jnp.exp(a_cs))
    dec_e = expand(jnp.exp(a_cs[L - 1:L, :] - a_cs))
    xdt = xs * dt_e
    xdt_b = xdt.astype(BF16)
    xdec_b = (xdt * dec_e).astype(BF16)
    lane_lo = lax.broadcasted_iota(I32, (L, 128), 1) < SSD_P

    hpg = SSD_HEADS // SSD_G
    gw = hpg * SSD_P
    y_diag, y_off = [], []
    for g in range(SSD_G):
        b0 = D + g * SSD_N
        c0 = D + SSD_G * SSD_N + g * SSD_N
        bm_b = xbc[:, b0:b0 + SSD_N].astype(BF16)
        cm_b = xbc[:, c0:c0 + SSD_N].astype(BF16)
        cb = _dot_nt(cm_b, bm_b)
        st = state[g]
        y_off.append(_dot(cm_b, st.astype(BF16)))
        for k in range(hpg // 2):
            pair = g * (hpg // 2) + k
            xp = xdt_b[:, pair * 128:(pair + 1) * 128]
            res = []
            for hh in range(2):
                h = pair * 2 + hh
                seg = a_cs[:, h:h + 1] - a_cs_t[h:h + 1, :]
                lmat = jnp.exp(jnp.where(causal, seg, -jnp.inf))
                res.append(_dot((cb * lmat).astype(BF16), xp))
            y_diag.append(jnp.where(lane_lo, res[0], res[1]))
        new = _dot_tn(bm_b, xdec_b[:, g * gw:(g + 1) * gw])
        state[g] = st * da_e[L - 1:L, g * gw:(g + 1) * gw] + new
    y = jnp.concatenate(y_diag, axis=1) + jnp.concatenate(y_off, axis=1) * da_e
    y = y + dsk_ref[...] * xs
    zz = z_ref[...]
    y = y * (zz * jax.nn.sigmoid(zz))
    o_ref[...] = _rms(y, nrm_ref[...]).astype(o_ref.dtype)


def _ssd(proj, dt, cw, cb, dtb, a, dsk, nrm, e, b, nc):
    t = proj.shape[0]
    full = lambda shape: pl.BlockSpec(shape, lambda i, j: (0,) * len(shape))
    return pl.pallas_call(
        _ssd_kernel,
        grid=(b, nc),
        in_specs=[
            pl.BlockSpec((CHUNK, D), lambda i, j: (i * nc + j, COL_XS)),
            pl.BlockSpec((CHUNK, 512), lambda i, j: (i * nc + j, COL_BC)),
            pl.BlockSpec((CHUNK, 128), lambda i, j: (i * nc + j, 0)),
            pl.BlockSpec((CHUNK, D), lambda i, j: (i * nc + j, COL_Z)),
            full((CONV_K, CONV_DIM)), full((1, CONV_DIM)), full((1, 128)), full((1, 128)),
            full((1, D)), full((1, D)), full((128, D)),
        ],
        out_specs=pl.BlockSpec((CHUNK, D), lambda i, j: (i * nc + j, 0)),
        out_shape=jax.ShapeDtypeStruct((t, D), BF16),
        scratch_shapes=[pltpu.VMEM((CHUNK + 8, CONV_DIM), F32),
                        pltpu.VMEM((SSD_G, SSD_N, (SSD_HEADS // SSD_G) * SSD_P), F32)],
        compiler_params=_cp(("parallel", "arbitrary")),
        name="ssd",
    )(proj, proj, dt, proj, cw, cb, dtb, a, dsk, nrm, e)


def _gmlp_kernel(u_ref, v_ref, nrm_ref, ws_ref, bs_ref, o_ref):
    L = CHUNK
    v = _rms(jax.nn.gelu(v_ref[...]), nrm_ref[...]).astype(BF16)
    row = lax.broadcasted_iota(I32, (L, L), 0)
    col = lax.broadcasted_iota(I32, (L, L), 1)
    causal = row >= col
    parts = []
    for g in range(GM_G):
        wg = jnp.where(causal, ws_ref[g], 0.0).astype(BF16)
        parts.append(_dot(wg, v[:, g * 128:(g + 1) * 128]))
    mixed = jnp.concatenate(parts, axis=1) + bs_ref[...]
    o_ref[...] = (jax.nn.gelu(u_ref[...]) * mixed).astype(o_ref.dtype)


def _gmlp(proj, nrm, ws, bs_e):
    t = proj.shape[0]
    return pl.pallas_call(
        _gmlp_kernel,
        grid=(t // CHUNK,),
        in_specs=[
            pl.BlockSpec((CHUNK, D), lambda i: (i, COL_U)),
            pl.BlockSpec((CHUNK, D), lambda i: (i, COL_V)),
            pl.BlockSpec((1, D), lambda i: (0, 0)),
            pl.BlockSpec((GM_G, CHUNK, CHUNK), lambda i: (0, 0, 0)),
            pl.BlockSpec((CHUNK, D), lambda i: (0, 0)),
        ],
        out_specs=pl.BlockSpec((CHUNK, D), lambda i: (i, 0)),
        out_shape=jax.ShapeDtypeStruct((t, D), BF16),
        compiler_params=_cp(("parallel",)),
        name="gmlp",
    )(proj, proj, nrm, ws, bs_e)


def _memkv_kernel(m_ref, g_ref, w_ref, o_ref):
    mn = _rms(m_ref[0], g_ref[...]).astype(BF16)
    o_ref[0] = _dot(mn, w_ref[...]).astype(o_ref.dtype)


def _memkv(mem, g, w):
    b, m, _ = mem.shape
    return pl.pallas_call(
        _memkv_kernel,
        grid=(b,),
        in_specs=[
            pl.BlockSpec((1, m, D), lambda i: (i, 0, 0)),
            pl.BlockSpec((1, D), lambda i: (0, 0)),
            pl.BlockSpec((D, 2 * D), lambda i: (0, 0)),
        ],
        out_specs=pl.BlockSpec((1, m, 2 * D), lambda i: (i, 0, 0)),
        out_shape=jax.ShapeDtypeStruct((b, m, 2 * D), BF16),
        compiler_params=_cp(("parallel",)),
        name="mem_kv",
    )(mem, g, w)


def _xattn_kernel(q_ref, kv_ref, o_ref):
    q = q_ref[...].astype(BF16)
    kv = kv_ref[0]
    outs = []
    for h in range(XA_H):
        qh = q[:, h * XA_D:(h + 1) * XA_D]
        kh = kv[:, h * XA_D:(h + 1) * XA_D]
        vh = kv[:, D + h * XA_D:D + (h + 1) * XA_D]
        s = _dot_nt(qh, kh) * (XA_D ** -0.5)
        s = s - jnp.max(s, axis=-1, keepdims=True)
        p = jnp.exp(s)
        p = p / jnp.sum(p, axis=-1, keepdims=True)
        outs.append(_dot(p.astype(BF16), vh))
    o_ref[...] = jnp.concatenate(outs, axis=1).astype(o_ref.dtype)


def _xattn(proj, kv, tq, s_len):
    t = proj.shape[0]
    m = kv.shape[1]
    per_b = s_len // tq
    return pl.pallas_call(
        _xattn_kernel,
        grid=(t // tq,),
        in_specs=[
            pl.BlockSpec((tq, D), lambda i: (i, COL_Q)),
            pl.BlockSpec((1, m, 2 * D), lambda i: (i // per_b, 0, 0)),
        ],
        out_specs=pl.BlockSpec((tq, D), lambda i: (i, 0)),
        out_shape=jax.ShapeDtypeStruct((t, D), BF16),
        compiler_params=_cp(("parallel",)),
        name="mem_xattn",
    )(proj, kv)


def _merge_kernel(x_ref, ys_ref, yg_ref, ym_ref, g0_ref, g1_ref, g2_ref, wb_ref, wo_ref,
                  lnf_ref, x1_ref, xn_ref):
    merged = jax.nn.sigmoid(g0_ref[...]) * _dot(ys_ref[...], wb_ref[0])
    merged = merged + jax.nn.sigmoid(g1_ref[...]) * _dot(yg_ref[...], wb_ref[1])
    merged = merged + jax.nn.sigmoid(g2_ref[...]) * _dot(ym_ref[...], wb_ref[2])
    x1 = x_ref[...] + _dot(merged.astype(BF16), wo_ref[...])
    x1_ref[...] = x1
    xn_ref[...] = _rms(x1, lnf_ref[...]).astype(xn_ref.dtype)


def _merge(x2, ys, yg, ym, proj, wb, wo, lnf, tm):
    t = x2.shape[0]
    tile = lambda cb: pl.BlockSpec((tm, D), lambda i: (i, cb))
    return pl.pallas_call(
        _merge_kernel,
        grid=(t // tm,),
        in_specs=[
            tile(0), tile(0), tile(0), tile(0),
            tile(COL_G), tile(COL_G + 1), tile(COL_G + 2),
            pl.BlockSpec((3, D, D), lambda i: (0, 0, 0)),
            pl.BlockSpec((D, D), lambda i: (0, 0)),
            pl.BlockSpec((1, D), lambda i: (0, 0)),
        ],
        out_specs=[tile(0), tile(0)],
        out_shape=[jax.ShapeDtypeStruct((t, D), F32), jax.ShapeDtypeStruct((t, D), BF16)],
        compiler_params=_cp(("parallel",)),
        name="merge",
    )(x2, ys, yg, ym, proj, proj, proj, wb, wo, lnf)


def _topk_rows(s, k, order=None, payload=None):
    if order is None:
        order = lax.broadcasted_iota(I32, s.shape, 0)
    big = jnp.int32(2 ** 30)
    vals, sel = [], []
    for _ in range(k):
        m = jnp.max(s, axis=0, keepdims=True)
        am = jnp.min(jnp.where(s == m, order, big), axis=0, keepdims=True)
        hit = order == am
        vals.append(m)
        if payload is None:
            sel.append(am)
        else:
            sel.append(jnp.max(jnp.where(hit, payload, -1), axis=0, keepdims=True))
        s = jnp.where(hit, -jnp.inf, s)
    return jnp.concatenate(vals, axis=0), jnp.concatenate(sel, axis=0)


def _pair_blocks():
    blocks = [("a", 0, 0), ("a", 0, 8)] + [("a", a, 0) for a in range(1, 8)] + [("b", 8, 0)]
    covered = set()
    for kind, p, q in blocks:
        covered |= {(p, q + r) if kind == "a" else (p + r, q) for r in range(8)}
    assert all((a, b) in covered for a in range(PEER_K) for b in range(PEER_K) if (a + 1) * (b + 1) <= PEER_K)
    return blocks


def _route_half(qt_ref, keys_ref, c):
    q = qt_ref[pl.ds(pl.multiple_of(c * 128, 128), 128), :].astype(BF16)
    sc = _dot(keys_ref[c], q)
    return _topk_rows(sc, PEER_K)


def _route_pairs(tv, ti):
    cand, cidx, order = [], [], []
    rows8 = lax.broadcasted_iota(I32, (8, 1), 0)
    for kind, p, q in _pair_blocks():
        if kind == "a":
            cand.append(tv[0][p:p + 1, :] + tv[1][q:q + 8, :])
            cidx.append(ti[0][p:p + 1, :] * PEER_KEYS + ti[1][q:q + 8, :])
            order.append(p * PEER_K + q + rows8)
        else:
            cand.append(tv[0][p:p + 8, :] + tv[1][q:q + 1, :])
            cidx.append(ti[0][p:p + 8, :] * PEER_KEYS + ti[1][q:q + 1, :])
            order.append((p + rows8) * PEER_K + q)
    cand = jnp.concatenate(cand, axis=0)
    order = jnp.broadcast_to(jnp.concatenate(order, axis=0), cand.shape)
    bv, be = _topk_rows(cand, PEER_K, order=order, payload=jnp.concatenate(cidx, axis=0))
    p = jnp.exp(bv - bv[0:1, :])
    return p / jnp.sum(p, axis=0, keepdims=True), be


PEER_TT = 128
PEER_SLOTS = 8
PEER_AHEAD = 6
PEER_RING = 3
TOK_SUB = 8
ROW_SUB = 2 * TOK_SUB


def _peer_kernel(xq_ref, wq_ref, keys_ref, uv_hbm, slot_hbm, x1_ref, lnf_ref, lno_ref, o_ref,
                 idx_smem, isem, gsem, qt_ref, tv_ref, ti_ref, eidt_ref, eid_ref, gate_ring,
                 xn_ref, y_ref, *bufs):
    s = pl.program_id(0)
    n = pl.num_programs(0) - 2
    tt = PEER_TT
    routing = s < n
    mixing = s >= 2
    tile = s - 2

    def idx_copy(tile_no):
        r = pl.multiple_of((tile_no % PEER_RING) * tt, tt)
        return pltpu.make_async_copy(eid_ref, idx_smem.at[pl.ds(r, tt), :], isem.at[0])

    per_part = PEER_SEL // TOK_SUB

    def issue(gtok, slot, part=None):
        row = gtok % (PEER_RING * tt)
        js = range(PEER_SEL) if part is None else range(part * per_part, (part + 1) * per_part)
        for j in js:
            e = idx_smem[row, j]
            pltpu.make_async_copy(uv_hbm.at[e], bufs[slot].at[:, j, :], gsem.at[slot]).start(priority=j % 2)

    def wait(slot):
        pltpu.make_async_copy(slot_hbm, bufs[slot], gsem.at[slot]).wait()

    @pl.when((s >= 1) & (s <= n))
    def _():
        idx_copy(s - 1).wait()

    @pl.when(s == 1)
    def _():
        for t in range(PEER_AHEAD):
            issue(t, t)

    @pl.when(routing)
    def _():
        qt_ref[...] = _dot_nt(wq_ref[...], xq_ref[...])

    def route_piece(c):
        half = c % 2
        v, ix = _route_half(qt_ref, keys_ref, c)
        tv_ref[half] = v
        ti_ref[half] = ix

        @pl.when(half == 1)
        def _():
            gate, be = _route_pairs([tv_ref[0], tv_ref[1]], [ti_ref[0], ti_ref[1]])
            r0 = pl.multiple_of((c // 2) * PEER_K, PEER_K)
            gate_ring[s % PEER_RING, pl.ds(r0, PEER_K), :] = gate
            eidt_ref[pl.ds(r0, PEER_K), :] = be

    def tile_rms(v, g):
        ms = jnp.sum(jnp.sum(v * v, axis=2, keepdims=True), axis=1, keepdims=True) * (1.0 / D)
        return v * lax.rsqrt(ms + EPS) * g

    lane = lax.broadcasted_iota(I32, (PEER_SEL, tt), 1)

    def gate_col(t):
        return jnp.sum(jnp.where(lane == t, gate_ring[tile % PEER_RING], 0.0), axis=1, keepdims=True)

    def value_row(slot, c, wb):
        return jnp.sum(bufs[slot][TOK_SUB + c] * wb, axis=0, keepdims=True)

    def mix_values(t, slot, wgt):
        wb = jnp.broadcast_to(wgt, (PEER_SEL, 128))
        rows = [value_row(slot, c, wb) for c in range(TOK_SUB)]
        y_ref[t] = x1_ref[t] + jnp.concatenate(rows, axis=0)

    def group(g, carry):
        @pl.when(routing)
        def _():
            route_piece(g)

        wgt_prev, gcol = carry
        for k in range(PEER_SLOTS):
            t = g * PEER_SLOTS + k
            wait(k)
            nxt, nxt_slot = tile * tt + t + PEER_AHEAD, (k + PEER_AHEAD) % PEER_SLOTS
            xn = xn_ref[t]
            wb = jnp.broadcast_to(wgt_prev, (PEER_SEL, 128))
            acc = jnp.zeros((PEER_SEL, 128), F32)
            rows = []
            for c in range(TOK_SUB):
                issue(nxt, nxt_slot, c)
                acc = acc + bufs[k][c] * xn[c:c + 1, :]
                if k > 0:
                    rows.append(value_row(k - 1, c, wb))
            a = jnp.sum(acc, axis=1, keepdims=True)
            wgt = gcol * jax.nn.gelu(a)
            gcol = gate_col(t + 1)
            if k > 0:
                y_ref[t - 1] = x1_ref[t - 1] + jnp.concatenate(rows, axis=0)
            else:
                @pl.when(g > 0)
                def _():
                    mix_values(t - 1, PEER_SLOTS - 1, wgt_prev)
            wgt_prev = wgt
        return wgt_prev, gcol

    n_groups = tt // PEER_SLOTS
    assert n_groups == PEER_H * 2

    @pl.when(mixing)
    def _():
        xn_ref[...] = tile_rms(x1_ref[...], lnf_ref[...])
        wgt_last, _ = lax.fori_loop(0, n_groups, group, (jnp.zeros((PEER_SEL, 1), F32), gate_col(0)))
        mix_values(tt - 1, PEER_SLOTS - 1, wgt_last)
        o_ref[...] = tile_rms(y_ref[...], lno_ref[...])

    @pl.when(jnp.logical_not(mixing))
    def _():
        def route_only(g, carry):
            route_piece(g)
            return carry

        lax.fori_loop(0, n_groups, route_only, 0)

    @pl.when(routing)
    def _():
        eid_ref[...] = eidt_ref[...].T
        idx_copy(s).start()

    @pl.when(s == n + 1)
    def _():
        for t in range(PEER_AHEAD):
            wait(t)


def _peer(xq, wqt, keys, uv, x1_3, lnf3, lno3):
    t = x1_3.shape[0]
    tt = PEER_TT
    assert tt % PEER_SLOTS == 0 and PEER_AHEAD < PEER_SLOTS and t % tt == 0 and t // tt >= PEER_RING
    n = t // tt
    slot_shape = (ROW_SUB, PEER_SEL, 128)
    mixed = lambda s: (jnp.maximum(s - 2, 0), 0, 0)
    return pl.pallas_call(
        _peer_kernel,
        grid=(n + 2,),
        in_specs=[
            pl.BlockSpec((tt, D), lambda s: (jnp.minimum(s, n - 1), 0)),
            pl.BlockSpec((PEER_H * PEER_QD, D), lambda s: (0, 0)),
            pl.BlockSpec((PEER_H * 2, PEER_KEYS, 128), lambda s: (0, 0, 0)),
            pl.BlockSpec(memory_space=pl.ANY),
            pl.BlockSpec(memory_space=pl.ANY),
            pl.BlockSpec((tt, TOK_SUB, 128), mixed),
            pl.BlockSpec((TOK_SUB, 128), lambda s: (0, 0)),
            pl.BlockSpec((TOK_SUB, 128), lambda s: (0, 0)),
        ],
        out_specs=pl.BlockSpec((tt, TOK_SUB, 128), mixed),
        out_shape=jax.ShapeDtypeStruct((t, TOK_SUB, 128), F32),
        scratch_shapes=[
            pltpu.SMEM((PEER_RING * tt, PEER_SEL), I32),
            pltpu.SemaphoreType.DMA((1,)),
            pltpu.SemaphoreType.DMA((PEER_SLOTS,)),
            pltpu.VMEM((PEER_H * PEER_QD, tt), F32),
            pltpu.VMEM((2, PEER_K, tt), F32),
            pltpu.VMEM((2, PEER_K, tt), I32),
            pltpu.VMEM((PEER_SEL, tt), I32),
            pltpu.VMEM((tt, PEER_SEL), I32),
            pltpu.VMEM((PEER_RING, PEER_SEL, tt), F32),
            pltpu.VMEM((tt, TOK_SUB, 128), F32),
            pltpu.VMEM((tt, TOK_SUB, 128), F32),
        ] + [pltpu.VMEM(slot_shape, F32) for _ in range(PEER_SLOTS)],
        compiler_params=_cp(("arbitrary",)),
        name="peer_route_experts",
    )(xq, wqt, keys, uv, jnp.zeros(slot_shape, F32), x1_3, lnf3, lno3)


def _pad_lanes(v, n=128):
    return jnp.pad(v, (0, n - v.shape[0])).reshape(1, n)


def _layer(x2, mem, b, s, ln_mix, w_in, conv_w, conv_b, dt_bias, a_log, d_skip, ssd_norm,
           gmlp_norm, w_spatial, b_spatial, ln_mem, w_mem_kv, w_branch, w_out, ln_ffn,
           w_peer_q, peer_keys, peer_u, peer_v, ln_out):
    t = x2.shape[0]
    nc = s // CHUNK
    bnd = [0, 1024, 2560, 2576, 3600, 4624, 5648, 8720]
    wz, wxbc, wdt, wu, wv, wq, wg = (w_in[:, bnd[k]:bnd[k + 1]] for k in range(7))
    w_main = jnp.concatenate([wz, wu, wv, wq, wg, wxbc], axis=1).astype(BF16)
    w_dt = jnp.pad(wdt, ((0, 0), (0, 128 - SSD_HEADS))).astype(BF16)
    tm = min(1024, t)
    proj, dt_raw = _inproj(x2, ln_mix.reshape(1, D), w_main, w_dt, tm, 2176)

    a_neg = _pad_lanes(-jnp.exp(a_log.astype(F32)))
    expand = (jnp.arange(128)[:, None] == (jnp.arange(D)[None, :] // SSD_P)).astype(F32)
    y_ssd = _ssd(proj, dt_raw, conv_w, conv_b.reshape(1, CONV_DIM), _pad_lanes(dt_bias.astype(F32)),
                 a_neg, jnp.repeat(d_skip, SSD_P).reshape(1, D), ssd_norm.reshape(1, D), expand, b, nc)

    bs_e = jnp.repeat(b_spatial.T, D // GM_G, axis=1)
    y_gmlp = _gmlp(proj, gmlp_norm.reshape(1, D), w_spatial, bs_e)

    kv = _memkv(mem, ln_mem.reshape(1, D), w_mem_kv.astype(BF16))
    y_mem = _xattn(proj, kv, min(512, s), s)

    x1, xn = _merge(x2, y_ssd, y_gmlp, y_mem, proj, w_branch.astype(BF16), w_out.astype(BF16),
                    ln_ffn.reshape(1, D), min(256, t))

    keys = peer_keys.reshape(PEER_H * 2, PEER_KEYS, 128).astype(BF16)
    n_exp = peer_u.shape[0]
    uv = jnp.concatenate([peer_u.reshape(n_exp, TOK_SUB, 128), peer_v.reshape(n_exp, TOK_SUB, 128)],
                         axis=1)
    out3 = _peer(xn, w_peer_q.T.astype(BF16), keys, uv, x1.reshape(t, TOK_SUB, 128),
                 ln_ffn.reshape(TOK_SUB, 128), ln_out.reshape(TOK_SUB, 128))
    return out3.reshape(t, D)


def kernel(x, mem, ln_mix, w_in, conv_w, conv_b, dt_bias, a_log, d_skip, ssd_norm, gmlp_norm,
           w_spatial, b_spatial, ln_mem, w_mem_kv, w_branch, w_out, ln_ffn, w_peer_q, peer_keys,
           peer_u, peer_v, ln_final):
    b, s, _ = x.shape
    depth = ln_mix.shape[0]
    assert depth == 1, "the PEER kernel fuses the final norm, so it supports a single layer"
    out = _layer(x.reshape(b * s, D), mem, b, s, ln_mix[0], w_in[0], conv_w[0], conv_b[0],
                 dt_bias[0], a_log[0], d_skip[0], ssd_norm[0], gmlp_norm[0], w_spatial[0],
                 b_spatial[0], ln_mem[0], w_mem_kv[0], w_branch[0], w_out[0], ln_ffn[0],
                 w_peer_q[0], peer_keys[0], peer_u[0], peer_v[0], ln_final)
    return out.reshape(b, s, D)
```

```python
import jax
import jax.numpy as jnp
from jax import lax
from jax.experimental import pallas as pl
from jax.experimental.pallas import tpu as pltpu

F32 = jnp.float32
BF16 = jnp.bfloat16
I32 = jnp.int32
HI = lax.Precision.HIGHEST

D = 1024
EPS = 1e-6
CHUNK = 128
SSD_HEADS = 16
SSD_P = 64
SSD_N = 128
SSD_G = 2
CONV_K = 4
CONV_DIM = 1536
GM_G = 8
XA_H = 4
XA_D = 256
PEER_H = 8
PEER_K = 16
PEER_KEYS = 128
PEER_SEL = PEER_H * PEER_K
PEER_QD = 256

PROJ_W = 8704
COL_Z, COL_U, COL_V, COL_Q, COL_G, COL_XS = 0, 1, 2, 3, 4, 7
COL_BC = 16

VMEM_LIMIT = 52 * 1024 * 1024


def _cp(sem):
    return pltpu.CompilerParams(dimension_semantics=sem, vmem_limit_bytes=VMEM_LIMIT)


def _rms(x, g):
    return x * lax.rsqrt(jnp.mean(x * x, axis=-1, keepdims=True) + EPS) * g


def _dot(a, b):
    return jnp.dot(a, b, preferred_element_type=F32)


def _dot_nt(a, b):
    return lax.dot_general(a, b, (((1,), (1,)), ((), ())), preferred_element_type=F32)


def _dot_tn(a, b):
    return lax.dot_general(a, b, (((0,), (0,)), ((), ())), preferred_element_type=F32)


def _inproj_kernel(x_ref, g_ref, w_ref, wdt_ref, proj_ref, dt_ref, h_ref):
    j = pl.program_id(1)

    @pl.when(j == 0)
    def _():
        h = _rms(x_ref[...], g_ref[...]).astype(BF16)
        h_ref[...] = h
        dt_ref[...] = _dot(h, wdt_ref[...])

    proj_ref[...] = _dot(h_ref[...], w_ref[...])


def _inproj(x2, g, w, wdt, tm, tn):
    t = x2.shape[0]
    return pl.pallas_call(
        _inproj_kernel,
        grid=(t // tm, PROJ_W // tn),
        in_specs=[
            pl.BlockSpec((tm, D), lambda i, j: (i, 0)),
            pl.BlockSpec((1, D), lambda i, j: (0, 0)),
            pl.BlockSpec((D, tn), lambda i, j: (0, j)),
            pl.BlockSpec((D, 128), lambda i, j: (0, 0)),
        ],
        out_specs=[
            pl.BlockSpec((tm, tn), lambda i, j: (i, j)),
            pl.BlockSpec((tm, 128), lambda i, j: (i, 0)),
        ],
        out_shape=[
            jax.ShapeDtypeStruct((t, PROJ_W), F32),
            jax.ShapeDtypeStruct((t, 128), F32),
        ],
        scratch_shapes=[pltpu.VMEM((tm, D), BF16)],
        compiler_params=_cp(("parallel", "arbitrary")),
        name="in_proj",
    )(x2, g, w, wdt)


def _ssd_kernel(xs_ref, bc_ref, dt_ref, z_ref, cw_ref, cb_ref, dtb_ref, a_ref, dsk_ref,
                nrm_ref, e_ref, o_ref, xbuf, state):
    c = pl.program_id(1)
    L = CHUNK

    @pl.when(c == 0)
    def _():
        xbuf[0:8, :] = jnp.zeros((8, CONV_DIM), F32)
        state[...] = jnp.zeros_like(state)

    xbuf[8:8 + L, 0:D] = xs_ref[...]
    xbuf[8:8 + L, D:CONV_DIM] = bc_ref[...]
    w = cw_ref[...]
    acc = cb_ref[...] + w[CONV_K - 1:CONV_K, :] * xbuf[8:8 + L, :]
    for s in range(1, CONV_K):
        acc = acc + w[CONV_K - 1 - s:CONV_K - s, :] * xbuf[8 - s:8 - s + L, :]
    xbuf[0:8, :] = xbuf[L:L + 8, :]
    xbc = acc * jax.nn.sigmoid(acc)
    xs = xbc[:, 0:D]

    dt = jax.nn.softplus(dt_ref[...] + dtb_ref[...])
    adt = dt * a_ref[...]
    row = lax.broadcasted_iota(I32, (L, L), 0)
    col = lax.broadcasted_iota(I32, (L, L), 1)
    causal = row >= col
    a_cs = jnp.dot(causal.astype(F32), adt, precision=HI, preferred_element_type=F32)
    a_cs_t = a_cs.T
    e = e_ref[...]

    def expand(v):
        return jnp.dot(v, e, precision=HI, preferred_element_type=F32)

    dt_e = expand(dt)
    da_e = expand(jnp.exp(a_cs))
    dec_e = expand(jnp.exp(a_cs[L - 1:L, :] - a_cs))
    xdt = xs * dt_e
    xdt_b = xdt.astype(BF16)
    xdec_b = (xdt * dec_e).astype(BF16)
    lane_lo = lax.broadcasted_iota(I32, (L, 128), 1) < SSD_P

    hpg = SSD_HEADS // SSD_G
    gw = hpg * SSD_P
    y_diag, y_off = [], []
    for g in range(SSD_G):
        b0 = D + g * SSD_N
        c0 = D + SSD_G * SSD_N + g * SSD_N
        bm_b = xbc[:, b0:b0 + SSD_N].astype(BF16)
        cm_b = xbc[:, c0:c0 + SSD_N].astype(BF16)
        cb = _dot_nt(cm_b, bm_b)
        st = state[g]
        y_off.append(_dot(cm_b, st.astype(BF16)))
        for k in range(hpg // 2):
            pair = g * (hpg // 2) + k
            xp = xdt_b[:, pair * 128:(pair + 1) * 128]
            res = []
            for hh in range(2):
                h = pair * 2 + hh
                seg = a_cs[:, h:h + 1] - a_cs_t[h:h + 1, :]
                lmat = jnp.exp(jnp.where(causal, seg, -jnp.inf))
                res.append(_dot((cb * lmat).astype(BF16), xp))
            y_diag.append(jnp.where(lane_lo, res[0], res[1]))
        new = _dot_tn(bm_b, xdec_b[:, g * gw:(g + 1) * gw])
        state[g] = st * da_e[L - 1:L, g * gw:(g + 1) * gw] + new
    y = jnp.concatenate(y_diag, axis=1) + jnp.concatenate(y_off, axis=1) * da_e
    y = y + dsk_ref[...] * xs
    zz = z_ref[...]
    y = y * (zz * jax.nn.sigmoid(zz))
    o_ref[...] = _rms(y, nrm_ref[...]).astype(o_ref.dtype)


def _ssd(proj, dt, cw, cb, dtb, a, dsk, nrm, e, b, nc):
    t = proj.shape[0]
    full = lambda shape: pl.BlockSpec(shape, lambda i, j: (0,) * len(shape))
    return pl.pallas_call(
        _ssd_kernel,
        grid=(b, nc),
        in_specs=[
            pl.BlockSpec((CHUNK, D), lambda i, j: (i * nc + j, COL_XS)),
            pl.BlockSpec((CHUNK, 512), lambda i, j: (i * nc + j, COL_BC)),
            pl.BlockSpec((CHUNK, 128), lambda i, j: (i * nc + j, 0)),
            pl.BlockSpec((CHUNK, D), lambda i, j: (i * nc + j, COL_Z)),
            full((CONV_K, CONV_DIM)), full((1, CONV_DIM)), full((1, 128)), full((1, 128)),
            full((1, D)), full((1, D)), full((128, D)),
        ],
        out_specs=pl.BlockSpec((CHUNK, D), lambda i, j: (i * nc + j, 0)),
        out_shape=jax.ShapeDtypeStruct((t, D), BF16),
        scratch_shapes=[pltpu.VMEM((CHUNK + 8, CONV_DIM), F32),
                        pltpu.VMEM((SSD_G, SSD_N, (SSD_HEADS // SSD_G) * SSD_P), F32)],
        compiler_params=_cp(("parallel", "arbitrary")),
        name="ssd",
    )(proj, proj, dt, proj, cw, cb, dtb, a, dsk, nrm, e)


def _gmlp_kernel(u_ref, v_ref, nrm_ref, ws_ref, bs_ref, o_ref):
    L = CHUNK
    v = _rms(jax.nn.gelu(v_ref[...]), nrm_ref[...]).astype(BF16)
    row = lax.broadcasted_iota(I32, (L, L), 0)
    col = lax.broadcasted_iota(I32, (L, L), 1)
    causal = row >= col
    parts = []
    for g in range(GM_G):
        wg = jnp.where(causal, ws_ref[g], 0.0).astype(BF16)
        parts.append(_dot(wg, v[:, g * 128:(g + 1) * 128]))
    mixed = jnp.concatenate(parts, axis=1) + bs_ref[...]
    o_ref[...] = (jax.nn.gelu(u_ref[...]) * mixed).astype(o_ref.dtype)


def _gmlp(proj, nrm, ws, bs_e):
    t = proj.shape[0]
    return pl.pallas_call(
        _gmlp_kernel,
        grid=(t // CHUNK,),
        in_specs=[
            pl.BlockSpec((CHUNK, D), lambda i: (i, COL_U)),
            pl.BlockSpec((CHUNK, D), lambda i: (i, COL_V)),
            pl.BlockSpec((1, D), lambda i: (0, 0)),
            pl.BlockSpec((GM_G, CHUNK, CHUNK), lambda i: (0, 0, 0)),
            pl.BlockSpec((CHUNK, D), lambda i: (0, 0)),
        ],
        out_specs=pl.BlockSpec((CHUNK, D), lambda i: (i, 0)),
        out_shape=jax.ShapeDtypeStruct((t, D), BF16),
        compiler_params=_cp(("parallel",)),
        name="gmlp",
    )(proj, proj, nrm, ws, bs_e)


def _memkv_kernel(m_ref, g_ref, w_ref, o_ref):
    mn = _rms(m_ref[0], g_ref[...]).astype(BF16)
    o_ref[0] = _dot(mn, w_ref[...]).astype(o_ref.dtype)


def _memkv(mem, g, w):
    b, m, _ = mem.shape
    return pl.pallas_call(
        _memkv_kernel,
        grid=(b,),
        in_specs=[
            pl.BlockSpec((1, m, D), lambda i: (i, 0, 0)),
            pl.BlockSpec((1, D), lambda i: (0, 0)),
            pl.BlockSpec((D, 2 * D), lambda i: (0, 0)),
        ],
        out_specs=pl.BlockSpec((1, m, 2 * D), lambda i: (i, 0, 0)),
        out_shape=jax.ShapeDtypeStruct((b, m, 2 * D), BF16),
        compiler_params=_cp(("parallel",)),
        name="mem_kv",
    )(mem, g, w)


def _xattn_kernel(q_ref, kv_ref, o_ref):
    q = q_ref[...].astype(BF16)
    kv = kv_ref[0]
    outs = []
    for h in range(XA_H):
        qh = q[:, h * XA_D:(h + 1) * XA_D]
        kh = kv[:, h * XA_D:(h + 1) * XA_D]
        vh = kv[:, D + h * XA_D:D + (h + 1) * XA_D]
        s = _dot_nt(qh, kh) * (XA_D ** -0.5)
        s = s - jnp.max(s, axis=-1, keepdims=True)
        p = jnp.exp(s)
        p = p / jnp.sum(p, axis=-1, keepdims=True)
        outs.append(_dot(p.astype(BF16), vh))
    o_ref[...] = jnp.concatenate(outs, axis=1).astype(o_ref.dtype)


def _xattn(proj, kv, tq, s_len):
    t = proj.shape[0]
    m = kv.shape[1]
    per_b = s_len // tq
    return pl.pallas_call(
        _xattn_kernel,
        grid=(t // tq,),
        in_specs=[
            pl.BlockSpec((tq, D), lambda i: (i, COL_Q)),
            pl.BlockSpec((1, m, 2 * D), lambda i: (i // per_b, 0, 0)),
        ],
        out_specs=pl.BlockSpec((tq, D), lambda i: (i, 0)),
        out_shape=jax.ShapeDtypeStruct((t, D), BF16),
        compiler_params=_cp(("parallel",)),
        name="mem_xattn",
    )(proj, kv)


def _merge_kernel(x_ref, ys_ref, yg_ref, ym_ref, g0_ref, g1_ref, g2_ref, wb_ref, wo_ref,
                  lnf_ref, x1_ref, xn_ref):
    merged = jax.nn.sigmoid(g0_ref[...]) * _dot(ys_ref[...], wb_ref[0])
    merged = merged + jax.nn.sigmoid(g1_ref[...]) * _dot(yg_ref[...], wb_ref[1])
    merged = merged + jax.nn.sigmoid(g2_ref[...]) * _dot(ym_ref[...], wb_ref[2])
    x1 = x_ref[...] + _dot(merged.astype(BF16), wo_ref[...])
    x1_ref[...] = x1
    xn_ref[...] = _rms(x1, lnf_ref[...]).astype(xn_ref.dtype)


def _merge(x2, ys, yg, ym, proj, wb, wo, lnf, tm):
    t = x2.shape[0]
    tile = lambda cb: pl.BlockSpec((tm, D), lambda i: (i, cb))
    return pl.pallas_call(
        _merge_kernel,
        grid=(t // tm,),
        in_specs=[
            tile(0), tile(0), tile(0), tile(0),
            tile(COL_G), tile(COL_G + 1), tile(COL_G + 2),
            pl.BlockSpec((3, D, D), lambda i: (0, 0, 0)),
            pl.BlockSpec((D, D), lambda i: (0, 0)),
            pl.BlockSpec((1, D), lambda i: (0, 0)),
        ],
        out_specs=[tile(0), tile(0)],
        out_shape=[jax.ShapeDtypeStruct((t, D), F32), jax.ShapeDtypeStruct((t, D), BF16)],
        compiler_params=_cp(("parallel",)),
        name="merge",
    )(x2, ys, yg, ym, proj, proj, proj, wb, wo, lnf)


def _topk_rows(s, k, order=None, payload=None):
    if order is None:
        order = lax.broadcasted_iota(I32, s.shape, 0)
    big = jnp.int32(2 ** 30)
    vals, sel = [], []
    for _ in range(k):
        m = jnp.max(s, axis=0, keepdims=True)
        am = jnp.min(jnp.where(s == m, order, big), axis=0, keepdims=True)
        hit = order == am
        vals.append(m)
        if payload is None:
            sel.append(am)
        else:
            sel.append(jnp.max(jnp.where(hit, payload, -1), axis=0, keepdims=True))
        s = jnp.where(hit, -jnp.inf, s)
    return jnp.concatenate(vals, axis=0), jnp.concatenate(sel, axis=0)


def _pair_blocks():
    blocks = [("a", 0, 0), ("a", 0, 8)] + [("a", a, 0) for a in range(1, 8)] + [("b", 8, 0)]
    covered = set()
    for kind, p, q in blocks:
        covered |= {(p, q + r) if kind == "a" else (p + r, q) for r in range(8)}
    assert all((a, b) in covered for a in range(PEER_K) for b in range(PEER_K) if (a + 1) * (b + 1) <= PEER_K)
    return blocks


def _route_kernel(xn_ref, wq_ref, keys_ref, eid_ref, gate_ref, qt_ref, eidt_ref):
    qt_ref[...] = _dot_nt(wq_ref[...], xn_ref[...])

    def head(h, carry):
        tv, ti = [], []
        for i in range(2):
            q = qt_ref[pl.ds(pl.multiple_of(h * PEER_QD + i * 128, 128), 128), :].astype(BF16)
            sc = _dot(keys_ref[h * 2 + i], q)
            v, ix = _topk_rows(sc, PEER_K)
            tv.append(v)
            ti.append(ix)
        cand, cidx, order = [], [], []
        rows8 = lax.broadcasted_iota(I32, (8, 1), 0)
        for kind, p, q in _pair_blocks():
            if kind == "a":
                cand.append(tv[0][p:p + 1, :] + tv[1][q:q + 8, :])
                cidx.append(ti[0][p:p + 1, :] * PEER_KEYS + ti[1][q:q + 8, :])
                order.append(p * PEER_K + q + rows8)
            else:
                cand.append(tv[0][p:p + 8, :] + tv[1][q:q + 1, :])
                cidx.append(ti[0][p:p + 8, :] * PEER_KEYS + ti[1][q:q + 1, :])
                order.append((p + rows8) * PEER_K + q)
        cand = jnp.concatenate(cand, axis=0)
        order = jnp.broadcast_to(jnp.concatenate(order, axis=0), cand.shape)
        bv, be = _topk_rows(cand, PEER_K, order=order, payload=jnp.concatenate(cidx, axis=0))
        p = jnp.exp(bv - bv[0:1, :])
        gate = p / jnp.sum(p, axis=0, keepdims=True)
        r0 = pl.multiple_of(h * PEER_K, PEER_K)
        gate_ref[pl.ds(r0, PEER_K), :] = gate
        eidt_ref[pl.ds(r0, PEER_K), :] = be
        return carry

    lax.fori_loop(0, PEER_H, head, 0)
    eid_ref[...] = eidt_ref[...].T


def _route(xn, wqt, keys, tr):
    t = xn.shape[0]
    return pl.pallas_call(
        _route_kernel,
        grid=(t // tr,),
        in_specs=[
            pl.BlockSpec((tr, D), lambda i: (i, 0)),
            pl.BlockSpec((PEER_H * PEER_QD, D), lambda i: (0, 0)),
            pl.BlockSpec((PEER_H * 2, PEER_KEYS, 128), lambda i: (0, 0, 0)),
        ],
        out_specs=[
            pl.BlockSpec((tr, PEER_SEL), lambda i: (i, 0)),
            pl.BlockSpec((PEER_SEL, tr), lambda i: (0, i)),
        ],
        out_shape=[jax.ShapeDtypeStruct((t, PEER_SEL), I32),
                   jax.ShapeDtypeStruct((PEER_SEL, t), F32)],
        scratch_shapes=[pltpu.VMEM((PEER_H * PEER_QD, tr), F32),
                        pltpu.VMEM((PEER_SEL, tr), I32)],
        compiler_params=_cp(("parallel",)),
        name="peer_route",
    )(xn, wqt, keys)


PEER_TT = 128
PEER_SLOTS = 8
PEER_AHEAD = 6
PEER_RING = 3
TOK_SUB = 8
ROW_SUB = 2 * TOK_SUB


def _peer_kernel(eid_hbm, uv_hbm, slot_hbm, x1_ref, gate_ref, lnf_ref, lno_ref, o_ref,
                 idx_smem, isem, gsem, xn_ref, y_ref, *bufs):
    i = pl.program_id(0)
    n = pl.num_programs(0)
    tt = PEER_TT
    nidx = tt * PEER_SEL

    def idx_copy(step):
        src = pl.multiple_of(jnp.minimum(step, n - 1) * nidx, nidx)
        r = step % PEER_RING
        return pltpu.make_async_copy(eid_hbm.at[pl.ds(src, nidx)],
                                     idx_smem.at[pl.ds(pl.multiple_of(r * nidx, nidx), nidx)], isem.at[r])

    per_part = PEER_SEL // TOK_SUB

    def issue(gtok, slot, part=None):
        base = (gtok % (PEER_RING * tt)) * PEER_SEL
        js = range(PEER_SEL) if part is None else range(part * per_part, (part + 1) * per_part)
        for j in js:
            e = idx_smem[base + j]
            pltpu.make_async_copy(uv_hbm.at[e], bufs[slot].at[:, j, :], gsem.at[slot]).start(priority=j % 2)

    def wait(slot):
        pltpu.make_async_copy(slot_hbm, bufs[slot], gsem.at[slot]).wait()

    @pl.when(i == 0)
    def _():
        idx_copy(0).start()
        idx_copy(1).start()
        idx_copy(0).wait()
        for t in range(PEER_AHEAD):
            issue(t, t)

    idx_copy(i + 1).wait()
    idx_copy(i + 2).start()

    def tile_rms(v, g):
        ms = jnp.sum(jnp.sum(v * v, axis=2, keepdims=True), axis=1, keepdims=True) * (1.0 / D)
        return v * lax.rsqrt(ms + EPS) * g

    xn_ref[...] = tile_rms(x1_ref[...], lnf_ref[...])
    lane = lax.broadcasted_iota(I32, (PEER_SEL, tt), 1)

    def gate_col(t):
        return jnp.sum(jnp.where(lane == t, gate_ref[...], 0.0), axis=1, keepdims=True)

    def value_row(slot, c, wb):
        return jnp.sum(bufs[slot][TOK_SUB + c] * wb, axis=0, keepdims=True)

    def mix_values(t, slot, wgt):
        wb = jnp.broadcast_to(wgt, (PEER_SEL, 128))
        rows = [value_row(slot, c, wb) for c in range(TOK_SUB)]
        y_ref[t] = x1_ref[t] + jnp.concatenate(rows, axis=0)

    def group(g, carry):
        wgt_prev, gcol = carry
        for k in range(PEER_SLOTS):
            t = g * PEER_SLOTS + k
            wait(k)
            nxt, nxt_slot = i * tt + t + PEER_AHEAD, (k + PEER_AHEAD) % PEER_SLOTS
            xn = xn_ref[t]
            wb = jnp.broadcast_to(wgt_prev, (PEER_SEL, 128))
            acc = jnp.zeros((PEER_SEL, 128), F32)
            rows = []
            for c in range(TOK_SUB):
                issue(nxt, nxt_slot, c)
                acc = acc + bufs[k][c] * xn[c:c + 1, :]
                if k > 0:
                    rows.append(value_row(k - 1, c, wb))
            a = jnp.sum(acc, axis=1, keepdims=True)
            wgt = gcol * jax.nn.gelu(a)
            gcol = gate_col(t + 1)
            if k > 0:
                y_ref[t - 1] = x1_ref[t - 1] + jnp.concatenate(rows, axis=0)
            else:
                @pl.when(g > 0)
                def _():
                    mix_values(t - 1, PEER_SLOTS - 1, wgt_prev)
            wgt_prev = wgt
        return wgt_prev, gcol

    wgt_last, _ = lax.fori_loop(0, tt // PEER_SLOTS, group,
                                (jnp.zeros((PEER_SEL, 1), F32), gate_col(0)))
    mix_values(tt - 1, PEER_SLOTS - 1, wgt_last)
    o_ref[...] = tile_rms(y_ref[...], lno_ref[...])

    @pl.when(i == n - 1)
    def _():
        idx_copy(i + 2).wait()
        for t in range(PEER_AHEAD):
            wait(t)


def _peer(eid_flat, uv, x1_3, gate_t, lnf3, lno3):
    t = x1_3.shape[0]
    tt = PEER_TT
    assert tt % PEER_SLOTS == 0 and PEER_AHEAD < PEER_SLOTS and t % tt == 0
    slot_shape = (ROW_SUB, PEER_SEL, 128)
    return pl.pallas_call(
        _peer_kernel,
        grid=(t // tt,),
        in_specs=[
            pl.BlockSpec(memory_space=pl.ANY),
            pl.BlockSpec(memory_space=pl.ANY),
            pl.BlockSpec(memory_space=pl.ANY),
            pl.BlockSpec((tt, TOK_SUB, 128), lambda i: (i, 0, 0)),
            pl.BlockSpec((PEER_SEL, tt), lambda i: (0, i)),
            pl.BlockSpec((TOK_SUB, 128), lambda i: (0, 0)),
            pl.BlockSpec((TOK_SUB, 128), lambda i: (0, 0)),
        ],
        out_specs=pl.BlockSpec((tt, TOK_SUB, 128), lambda i: (i, 0, 0)),
        out_shape=jax.ShapeDtypeStruct((t, TOK_SUB, 128), F32),
        scratch_shapes=[
            pltpu.SMEM((PEER_RING * tt * PEER_SEL,), I32),
            pltpu.SemaphoreType.DMA((PEER_RING,)),
            pltpu.SemaphoreType.DMA((PEER_SLOTS,)),
            pltpu.VMEM((tt, TOK_SUB, 128), F32),
            pltpu.VMEM((tt, TOK_SUB, 128), F32),
        ] + [pltpu.VMEM(slot_shape, F32) for _ in range(PEER_SLOTS)],
        compiler_params=_cp(("arbitrary",)),
        name="peer_experts",
    )(eid_flat, uv, jnp.zeros(slot_shape, F32), x1_3, gate_t, lnf3, lno3)


def _pad_lanes(v, n=128):
    return jnp.pad(v, (0, n - v.shape[0])).reshape(1, n)


def _layer(x2, mem, b, s, ln_mix, w_in, conv_w, conv_b, dt_bias, a_log, d_skip, ssd_norm,
           gmlp_norm, w_spatial, b_spatial, ln_mem, w_mem_kv, w_branch, w_out, ln_ffn,
           w_peer_q, peer_keys, peer_u, peer_v, ln_out):
    t = x2.shape[0]
    nc = s // CHUNK
    bnd = [0, 1024, 2560, 2576, 3600, 4624, 5648, 8720]
    wz, wxbc, wdt, wu, wv, wq, wg = (w_in[:, bnd[k]:bnd[k + 1]] for k in range(7))
    w_main = jnp.concatenate([wz, wu, wv, wq, wg, wxbc], axis=1).astype(BF16)
    w_dt = jnp.pad(wdt, ((0, 0), (0, 128 - SSD_HEADS))).astype(BF16)
    tm = min(1024, t)
    proj, dt_raw = _inproj(x2, ln_mix.reshape(1, D), w_main, w_dt, tm, 2176)

    a_neg = _pad_lanes(-jnp.exp(a_log.astype(F32)))
    expand = (jnp.arange(128)[:, None] == (jnp.arange(D)[None, :] // SSD_P)).astype(F32)
    y_ssd = _ssd(proj, dt_raw, conv_w, conv_b.reshape(1, CONV_DIM), _pad_lanes(dt_bias.astype(F32)),
                 a_neg, jnp.repeat(d_skip, SSD_P).reshape(1, D), ssd_norm.reshape(1, D), expand, b, nc)

    bs_e = jnp.repeat(b_spatial.T, D // GM_G, axis=1)
    y_gmlp = _gmlp(proj, gmlp_norm.reshape(1, D), w_spatial, bs_e)

    kv = _memkv(mem, ln_mem.reshape(1, D), w_mem_kv.astype(BF16))
    y_mem = _xattn(proj, kv, min(512, s), s)

    x1, xn = _merge(x2, y_ssd, y_gmlp, y_mem, proj, w_branch.astype(BF16), w_out.astype(BF16),
                    ln_ffn.reshape(1, D), min(256, t))

    keys = peer_keys.reshape(PEER_H * 2, PEER_KEYS, 128).astype(BF16)
    eid, gate_t = _route(xn, w_peer_q.T.astype(BF16), keys, min(512, t))

    n_exp = peer_u.shape[0]
    uv = jnp.concatenate([peer_u.reshape(n_exp, TOK_SUB, 128), peer_v.reshape(n_exp, TOK_SUB, 128)],
                         axis=1)
    out3 = _peer(eid.reshape(-1), uv, x1.reshape(t, TOK_SUB, 128), gate_t,
                 ln_ffn.reshape(TOK_SUB, 128), ln_out.reshape(TOK_SUB, 128))
    return out3.reshape(t, D)


def kernel(x, mem, ln_mix, w_in, conv_w, conv_b, dt_bias, a_log, d_skip, ssd_norm, gmlp_norm,
           w_spatial, b_spatial, ln_mem, w_mem_kv, w_branch, w_out, ln_ffn, w_peer_q, peer_keys,
           peer_u, peer_v, ln_final):
    b, s, _ = x.shape
    depth = ln_mix.shape[0]
    assert depth == 1, "the PEER kernel fuses the final norm, so it supports a single layer"
    out = _layer(x.reshape(b * s, D), mem, b, s, ln_mix[0], w_in[0], conv_w[0], conv_b[0],
                 dt_bias[0], a_log[0], d_skip[0], ssd_norm[0], gmlp_norm[0], w_spatial[0],
                 b_spatial[0], ln_mem[0], w_mem_kv[0], w_branch[0], w_out[0], ln_ffn[0],
                 w_peer_q[0], peer_keys[0], peer_u[0], peer_v[0], ln_final)
    return out.reshape(b, s, D)
```
